```python
import math
import jax, jax.numpy as jnp
from jax import lax
import numpy as np

D_MODEL = 1024
BATCH = 16
SEQ = 4096
DEPTH = 2

HEAD_DIM = 64
BRANCH_WIDTH = D_MODEL
N_BRANCH = 3
SSM_INNER = BRANCH_WIDTH
SSM_HEAD_DIM = 64
SSM_HEADS = SSM_INNER // SSM_HEAD_DIM
SSM_GROUPS = 4
SSM_HEADS_PER_GROUP = SSM_HEADS // SSM_GROUPS
SSM_STATE = 128
SSM_CONV_DIM = SSM_INNER + 2 * SSM_GROUPS * SSM_STATE
CONV_WIDTH = 4
SSD_CHUNK = 128
SWA_HEADS = BRANCH_WIDTH // HEAD_DIM
SWA_KV_HEADS = 4
SWA_GROUP = SWA_HEADS // SWA_KV_HEADS
SWA_WINDOW = 128
SWA_BLOCK = 128
FOX_HEADS = BRANCH_WIDTH // HEAD_DIM
FOX_BLOCK = 128
ROPE_THETA = 10000.0
NORM_EPS = 1e-6
IN_SIZES = (SSM_CONV_DIM, SSM_INNER, SSM_HEADS,
            SWA_HEADS * HEAD_DIM, SWA_KV_HEADS * HEAD_DIM, SWA_KV_HEADS * HEAD_DIM, BRANCH_WIDTH,
            FOX_HEADS * HEAD_DIM, FOX_HEADS * HEAD_DIM, FOX_HEADS * HEAD_DIM, FOX_HEADS, BRANCH_WIDTH,
            N_BRANCH * D_MODEL)
N_IN = sum(IN_SIZES)

kernel_name = "hybrid_ssd_swa_fox_gated_block"


def rms_norm(x, w):
    xf = x.astype(jnp.float32)
    y = xf * lax.rsqrt(jnp.mean(xf * xf, axis=-1, keepdims=True) + NORM_EPS)
    return (y * w.astype(jnp.float32)).astype(x.dtype)


def grouped_rms_norm(y, w, groups):
    b, s, d = y.shape
    yf = y.astype(jnp.float32).reshape(b, s, groups, d // groups)
    yf = yf * lax.rsqrt(jnp.mean(yf * yf, axis=-1, keepdims=True) + NORM_EPS)
    return (yf.reshape(b, s, d) * w.astype(jnp.float32)).astype(y.dtype)


def rope(x, cos, sin):
    half = x.shape[-1] // 2
    x1, x2 = x[..., :half], x[..., half:]
    c, s = cos[None, :, None, :], sin[None, :, None, :]
    return jnp.concatenate([x1 * c - x2 * s, x2 * c + x1 * s], axis=-1)


def causal_depthwise_conv(u, w, bias):
    c = u.shape[-1]
    out = lax.conv_general_dilated(u, w[:, None, :], window_strides=(1,),
                                   padding=[(CONV_WIDTH - 1, 0)],
                                   dimension_numbers=('NWC', 'WIO', 'NWC'),
                                   feature_group_count=c)
    return out + bias


def ssd_chunked_scan(xh, dt, a, bm, cm):
    b, s, g, r, p = xh.shape
    n = bm.shape[-1]
    nc, l = s // SSD_CHUNK, SSD_CHUNK
    dtype = xh.dtype
    x = xh.reshape(b, nc, l, g, r, p)
    dtc = dt.reshape(b, nc, l, g, r)
    x_dt = x * dtc[..., None]
    bc = bm.reshape(b, nc, l, g, n)
    cc = cm.reshape(b, nc, l, g, n)
    a_dt = jnp.transpose(dtc.astype(jnp.float32) * a.astype(jnp.float32), (0, 3, 4, 1, 2))
    a_cum = jnp.cumsum(a_dt, axis=-1)
    idx = jnp.arange(l)
    causal = idx[:, None] >= idx[None, :]
    seg = a_cum[..., :, None] - a_cum[..., None, :]
    decay = jnp.where(causal, jnp.exp(jnp.where(causal, seg, 0.0)), 0.0).astype(dtype)
    cb = jnp.einsum('bclgn,bcsgn->bgcls', cc, bc)
    y_diag = jnp.einsum('bgrcls,bcsgrp->bclgrp', cb[:, :, None] * decay, x_dt)
    decay_states = jnp.exp(a_cum[..., -1:] - a_cum).astype(dtype)
    states = jnp.einsum('bcsgn,bgrcs,bcsgrp->bcgrpn', bc, decay_states, x_dt)
    chunk_decay = jnp.exp(a_cum[..., -1]).astype(dtype)

    def step(h, inp):
        st, dec = inp
        return h * dec[..., None, None] + st, h

    h0 = jnp.zeros((b, g, r, p, n), dtype)
    _, prev = lax.scan(step, h0, (jnp.moveaxis(states, 1, 0), jnp.moveaxis(chunk_decay, -1, 0)))
    prev = jnp.moveaxis(prev, 0, 1)
    y_off = jnp.einsum('bclgn,bcgrpn,bgrcl->bclgrp', cc, prev, jnp.exp(a_cum).astype(dtype))
    return (y_diag + y_off).reshape(b, s, g, r, p)


def mamba2_branch(xbc, z, dt_raw, conv_w, conv_b, dt_bias, a_log, d_skip, norm_w):
    b, s, _ = xbc.shape
    xbc = jax.nn.silu(causal_depthwise_conv(xbc, conv_w, conv_b))
    gn = SSM_GROUPS * SSM_STATE
    xs, bm, cm = jnp.split(xbc, [SSM_INNER, SSM_INNER + gn], axis=-1)
    xh = xs.reshape(b, s, SSM_GROUPS, SSM_HEADS_PER_GROUP, SSM_HEAD_DIM)
    dt = jax.nn.softplus(dt_raw + dt_bias).reshape(b, s, SSM_GROUPS, SSM_HEADS_PER_GROUP)
    a = -jnp.exp(a_log).reshape(SSM_GROUPS, SSM_HEADS_PER_GROUP)
    y = ssd_chunked_scan(xh, dt, a,
                         bm.reshape(b, s, SSM_GROUPS, SSM_STATE),
                         cm.reshape(b, s, SSM_GROUPS, SSM_STATE))
    y = y + d_skip.reshape(SSM_GROUPS, SSM_HEADS_PER_GROUP)[:, :, None] * xh
    y = y.reshape(b, s, SSM_INNER) * jax.nn.silu(z)
    return grouped_rms_norm(y, norm_w, SSM_GROUPS)


def sliding_window_branch(q, k, v, z, sinks, cos, sin):
    b, s, _ = q.shape
    nb, blk = s // SWA_BLOCK, SWA_BLOCK
    q = rope(q.reshape(b, s, SWA_HEADS, HEAD_DIM), cos, sin)
    k = rope(k.reshape(b, s, SWA_KV_HEADS, HEAD_DIM), cos, sin)
    v = v.reshape(b, s, SWA_KV_HEADS, HEAD_DIM)
    qb = q.reshape(b, nb, blk, SWA_KV_HEADS, SWA_GROUP, HEAD_DIM)
    pad = ((0, 0), (blk, 0), (0, 0), (0, 0))
    kp, vp = jnp.pad(k, pad)[:, :s], jnp.pad(v, pad)[:, :s]
    kb = jnp.concatenate([kp.reshape(b, nb, blk, SWA_KV_HEADS, HEAD_DIM),
                          k.reshape(b, nb, blk, SWA_KV_HEADS, HEAD_DIM)], axis=2)
    vb = jnp.concatenate([vp.reshape(b, nb, blk, SWA_KV_HEADS, HEAD_DIM),
                          v.reshape(b, nb, blk, SWA_KV_HEADS, HEAD_DIM)], axis=2)
    scores = jnp.einsum('bnqkgd,bnskd->bnkgqs', qb, kb).astype(jnp.float32) * (HEAD_DIM ** -0.5)
    qi = jnp.arange(blk)[:, None]
    sj = jnp.arange(2 * blk)[None, :]
    diff = qi + blk - sj
    band = (diff >= 0) & (diff < SWA_WINDOW)
    key_pos = jnp.arange(nb)[:, None, None] * blk - blk + sj[None]
    mask = band[None] & (key_pos >= 0)
    scores = jnp.where(mask[None, :, None, None], scores, -jnp.inf)
    sink = jnp.broadcast_to(sinks.astype(jnp.float32).reshape(1, 1, SWA_KV_HEADS, SWA_GROUP, 1, 1),
                            scores.shape[:-1] + (1,))
    probs = jax.nn.softmax(jnp.concatenate([scores, sink], axis=-1), axis=-1)[..., :-1]
    out = jnp.einsum('bnkgqs,bnskd->bnqkgd', probs.astype(v.dtype), vb)
    return out.reshape(b, s, SWA_HEADS * HEAD_DIM) * jax.nn.silu(z)


def forgetting_attention_branch(q, k, v, f_logit, z, f_bias):
    b, s, _ = q.shape
    q = q.reshape(b, s, FOX_HEADS, HEAD_DIM)
    k = k.reshape(b, s, FOX_HEADS, HEAD_DIM)
    v = v.reshape(b, s, FOX_HEADS, HEAD_DIM)
    log_f = jax.nn.log_sigmoid((f_logit + f_bias).astype(jnp.float32))
    cum = jnp.transpose(jnp.cumsum(log_f, axis=1), (0, 2, 1))
    scale = HEAD_DIM ** -0.5
    outs = []
    for i in range(s // FOX_BLOCK):
        start, end = i * FOX_BLOCK, (i + 1) * FOX_BLOCK
        sc = jnp.einsum('bqhd,bkhd->bhqk', q[:, start:end], k[:, :end]).astype(jnp.float32) * scale
        sc = sc + cum[:, :, start:end, None] - cum[:, :, None, :end]
        causal = (start + jnp.arange(FOX_BLOCK))[:, None] >= jnp.arange(end)[None, :]
        sc = jnp.where(causal[None, None], sc, -jnp.inf)
        probs = jax.nn.softmax(sc, axis=-1).astype(v.dtype)
        outs.append(jnp.einsum('bhqk,bkhd->bqhd', probs, v[:, :end]))
    out = jnp.concatenate(outs, axis=1)
    return out.reshape(b, s, FOX_HEADS * HEAD_DIM) * jax.nn.silu(z)


def setup_inputs(seed: int = 0) -> dict:
    key = jax.random.key(seed)
    ks = jax.random.split(key, 16)
    L, D, W = DEPTH, D_MODEL, BRANCH_WIDTH
    x = jax.random.normal(ks[0], (BATCH, SEQ, D), jnp.float32)
    norm_w = 1.0 + 0.1 * jax.random.normal(ks[1], (L, D), jnp.float32)
    w_in = jax.random.normal(ks[2], (L, D, N_IN), jnp.float32) * D ** -0.5
    conv_w = jax.random.normal(ks[3], (L, CONV_WIDTH, SSM_CONV_DIM), jnp.float32) * CONV_WIDTH ** -0.5
    conv_b = 0.01 * jax.random.normal(ks[4], (L, SSM_CONV_DIM), jnp.float32)
    u = jax.random.uniform(ks[5], (L, SSM_HEADS), jnp.float32)
    dt0 = jnp.exp(u * (math.log(0.1) - math.log(0.001)) + math.log(0.001))
    dt_bias = dt0 + jnp.log(-jnp.expm1(-dt0))
    a_log = jnp.log(jax.random.uniform(ks[6], (L, SSM_HEADS), jnp.float32, minval=1.0, maxval=16.0))
    d_skip = 1.0 + 0.1 * jax.random.normal(ks[7], (L, SSM_HEADS), jnp.float32)
    ssm_norm_w = 1.0 + 0.1 * jax.random.normal(ks[8], (L, SSM_INNER), jnp.float32)
    sinks = 0.5 * jax.random.normal(ks[9], (L, SWA_HEADS), jnp.float32)
    f_bias = 3.0 + 0.5 * jax.random.normal(ks[10], (L, FOX_HEADS), jnp.float32)
    gate_bias = 0.1 * jax.random.normal(ks[11], (L, N_BRANCH, D), jnp.float32)
    w_proj = jax.random.normal(ks[12], (L, N_BRANCH, W, D), jnp.float32) * W ** -0.5
    w_out = jax.random.normal(ks[13], (L, D, D), jnp.float32) * D ** -0.5
    final_norm_w = 1.0 + 0.1 * jax.random.normal(ks[14], (D,), jnp.float32)
    return {"x": x, "norm_w": norm_w, "w_in": w_in, "conv_w": conv_w, "conv_b": conv_b,
            "dt_bias": dt_bias, "a_log": a_log, "d_skip": d_skip, "ssm_norm_w": ssm_norm_w,
            "sinks": sinks, "f_bias": f_bias, "gate_bias": gate_bias, "w_proj": w_proj,
            "w_out": w_out, "final_norm_w": final_norm_w}


def reference(x, norm_w, w_in, conv_w, conv_b, dt_bias, a_log, d_skip, ssm_norm_w,
              sinks, f_bias, gate_bias, w_proj, w_out, final_norm_w):
    b, s, d = x.shape
    pos = jnp.arange(s, dtype=jnp.float32)
    inv_freq = ROPE_THETA ** (-jnp.arange(0, HEAD_DIM, 2, dtype=jnp.float32) / HEAD_DIM)
    ang = pos[:, None] * inv_freq[None, :]
    cos, sin = jnp.cos(ang).astype(x.dtype), jnp.sin(ang).astype(x.dtype)
    split_at = [int(v) for v in np.cumsum(IN_SIZES)[:-1]]
    for layer in range(DEPTH):
        h = rms_norm(x, norm_w[layer])
        proj = jnp.einsum('bsd,de->bse', h, w_in[layer])
        (a_xbc, a_z, a_dt, b_q, b_k, b_v, b_z,
         c_q, c_k, c_v, c_f, c_z, gates) = jnp.split(proj, split_at, axis=-1)
        y_a = mamba2_branch(a_xbc, a_z, a_dt, conv_w[layer], conv_b[layer], dt_bias[layer],
                            a_log[layer], d_skip[layer], ssm_norm_w[layer])
        y_b = sliding_window_branch(b_q, b_k, b_v, b_z, sinks[layer], cos, sin)
        y_c = forgetting_attention_branch(c_q, c_k, c_v, c_f, c_z, f_bias[layer])
        ys = jnp.stack([y_a, y_b, y_c], axis=2)
        branch = jnp.einsum('bsiw,iwd->bsid', ys, w_proj[layer])
        g = jax.nn.sigmoid(gates.reshape(b, s, N_BRANCH, d) + gate_bias[layer])
        merged = jnp.sum(g * branch, axis=2)
        x = x + jnp.einsum('bsd,de->bse', merged, w_out[layer])
    return rms_norm(x, final_norm_w)
```

```python
import functools
import math

import jax
import jax.numpy as jnp
import numpy as np
from jax import lax
from jax.experimental import pallas as pl
from jax.experimental.pallas import tpu as pltpu

F32 = jnp.float32
BF16 = jnp.bfloat16

D_MODEL = 1024
HEAD_DIM = 64
N_HEADS = 16
N_PAIRS = N_HEADS // 2
LANES = 128
SSM_GROUPS = 4
SSM_STATE = 128
CONV_WIDTH = 4
CHUNK = 128
SWA_KV_HEADS = 4
ROPE_THETA = 10000.0
NORM_EPS = 1e-6
NEG = -1e30

COL_A_XBC = 0
COL_A_Z = 2048
COL_B_Z = 3072
COL_C_Z = 4096
COL_B_Q = 5120
COL_C_Q = 6144
COL_C_K = 7168
COL_C_V = 8192
COL_GATES = 9216
COL_B_K = 12288
COL_B_V = 12800
N_PROJ = 13312

_IN_SIZES = (2048, 1024, 16, 1024, 256, 256, 1024, 1024, 1024, 1024, 16, 1024, 3072)

VMEM_LIMIT = 56 * 1024 * 1024


def _cparams(n_axes):
    return pltpu.CompilerParams(dimension_semantics=("arbitrary",) * n_axes, vmem_limit_bytes=VMEM_LIMIT)


def _split3(x):
    hi = x.astype(BF16)
    r1 = x - hi.astype(F32)
    mid = r1.astype(BF16)
    lo = (r1 - mid.astype(F32)).astype(BF16)
    return hi, mid, lo


def _dot(a, b):
    return jnp.dot(a, b, preferred_element_type=F32)


def _dot_nt(a, b):
    return lax.dot_general(a, b, (((1,), (1,)), ((), ())), preferred_element_type=F32)


def _silu(x):
    return x * (1.0 / (1.0 + jnp.exp(-x)))


def _softplus(x):
    return jnp.maximum(x, 0.0) + jnp.log1p(jnp.exp(-jnp.abs(x)))


IN_TM = 1024
IN_TN = 1024


def _inproj_kernel(x_ref, nw_ref, w_ref, wsm_ref, o_ref, osm_ref, h_ref):
    @pl.when(pl.program_id(1) == 0)
    def _():
        x = x_ref[...]
        ms = jnp.mean(x * x, axis=-1, keepdims=True)
        h = x * lax.rsqrt(ms + NORM_EPS) * nw_ref[...]
        hh = h.astype(BF16)
        h_ref[...] = hh
        hl = (h - hh.astype(F32)).astype(BF16)
        osm_ref[...] = _dot(hh, wsm_ref[0]) + _dot(hl, wsm_ref[0]) + _dot(hh, wsm_ref[1])

    o_ref[...] = _dot(h_ref[...], w_ref[...]).astype(BF16)


def _inproj(x2, norm_w, w_big, w_small):
    t = x2.shape[0]
    grid = (t // IN_TM, N_PROJ // IN_TN)
    return pl.pallas_call(
        _inproj_kernel,
        grid=grid,
        in_specs=[
            pl.BlockSpec((IN_TM, D_MODEL), lambda i, j: (i, 0)),
            pl.BlockSpec((1, D_MODEL), lambda i, j: (0, 0)),
            pl.BlockSpec((D_MODEL, IN_TN), lambda i, j: (0, j)),
            pl.BlockSpec((2, D_MODEL, LANES), lambda i, j: (0, 0, 0)),
        ],
        out_specs=[
            pl.BlockSpec((IN_TM, IN_TN), lambda i, j: (i, j)),
            pl.BlockSpec((IN_TM, LANES), lambda i, j: (i, 0)),
        ],
        out_shape=[
            jax.ShapeDtypeStruct((t, N_PROJ), BF16),
            jax.ShapeDtypeStruct((t, LANES), F32),
        ],
        scratch_shapes=[pltpu.VMEM((IN_TM, D_MODEL), BF16)],
        compiler_params=_cparams(2),
        name="inproj",
    )(x2, norm_w, w_big, w_small)


LANE_F = 16


def _scan_selectors():
    selk = np.zeros((3 * LANES, N_PAIRS * LANES), np.float32)
    konst = np.zeros((1, N_PAIRS * LANES), np.float32)
    selq = np.zeros((LANES, 3 * LANES), np.float32)
    qonst = np.zeros((LANES, 1), np.float32)
    for p in range(N_PAIRS):
        for e in range(2):
            src = LANE_F + 2 * p + e
            kbase = p * LANES + (64 if e == 0 else 0)
            qbase = 16 * p + 8 * e
            for t in range(3):
                konst[0, kbase + t] = 1.0
                selk[t * LANES + src, kbase + 3 + t] = 1.0
                selq[qbase + t, t * LANES + src] = 1.0
                qonst[qbase + 3 + t, 0] = -1.0
    return selk, konst, selq, qonst


def _scan_kernel(sm_ref, fb_ref, selk_ref, konst_ref, selq_ref, qonst_ref, kf_ref, qf_ref):
    s = sm_ref.shape[0]
    row = lax.broadcasted_iota(jnp.int32, (CHUNK, CHUNK), 0)
    col = lax.broadcasted_iota(jnp.int32, (CHUNK, CHUNK), 1)
    tri = (row >= col).astype(BF16)
    fb = fb_ref[...]

    def body(c, carry):
        rows = pl.ds(pl.multiple_of(c * CHUNK, CHUNK), CHUNK)
        v = sm_ref[rows, :] + fb
        lf = jnp.minimum(v, 0.0) - jnp.log1p(jnp.exp(-jnp.abs(v)))
        hi, mid, lo = _split3(lf)
        local = _dot(tri, jnp.concatenate([hi, mid, lo], axis=1))
        cum = local[:, :LANES] + local[:, LANES:2 * LANES] + local[:, 2 * LANES:] + carry
        c3 = jnp.concatenate(_split3(cum), axis=1)
        kf_ref[rows, :] = (_dot(c3, selk_ref[...]) + konst_ref[...]).astype(BF16)
        qf_ref[c] = _dot_nt(selq_ref[...], c3) + qonst_ref[...]
        return cum[CHUNK - 1:CHUNK, :]

    lax.fori_loop(0, s // CHUNK, body, jnp.zeros((1, LANES), F32))


def _fox_scan(small, fb_row, seq):
    t = small.shape[0]
    b = t // seq
    selk, konst, selq, qonst = _scan_selectors()
    const = lambda shape: pl.BlockSpec(shape, lambda i: (0,) * len(shape))
    return pl.pallas_call(
        _scan_kernel,
        grid=(b,),
        in_specs=[
            pl.BlockSpec((seq, LANES), lambda i: (i, 0)),
            const((1, LANES)),
            const(selk.shape), const(konst.shape), const(selq.shape), const(qonst.shape),
        ],
        out_specs=[
            pl.BlockSpec((seq, N_PAIRS * LANES), lambda i: (i, 0)),
            pl.BlockSpec((None, seq // CHUNK, LANES, CHUNK), lambda i: (i, 0, 0, 0)),
        ],
        out_shape=[
            jax.ShapeDtypeStruct((t, N_PAIRS * LANES), BF16),
            jax.ShapeDtypeStruct((b, seq // CHUNK, LANES, CHUNK), F32),
        ],
        compiler_params=_cparams(1),
        name="fox_scan",
    )(small, fb_row, jnp.asarray(selk, BF16), jnp.asarray(konst, F32), jnp.asarray(selq, BF16),
      jnp.asarray(qonst, F32))


FOX_BLK = 256


def _fox_kernel(q_ref, k_ref, v_ref, kf_ref, qf_ref, z_ref, o_ref, ka_ref, kb_ref, vt_ref):
    qi = pl.program_id(2)
    seq = k_ref.shape[0]
    nb = FOX_BLK

    @pl.when(qi == 0)
    def _prologue():
        lane = lax.broadcasted_iota(jnp.int32, (nb, LANES), 1)
        for c in range(seq // nb):
            rows = slice(c * nb, (c + 1) * nb)
            k2 = k_ref[rows, :]
            kf = kf_ref[rows, :]
            ka_ref[rows, :] = jnp.where(lane < HEAD_DIM, k2, kf)
            kb_ref[rows, :] = jnp.where(lane >= HEAD_DIM, k2, kf)
            vt_ref[c] = v_ref[rows, :].astype(F32).T.astype(BF16)

    qt = q_ref[...].astype(F32).T
    qf = jnp.concatenate([qf_ref[c] for c in range(nb // CHUNK)], axis=1)
    zeros56 = jnp.zeros((HEAD_DIM - 8, nb), F32)
    qta = jnp.concatenate([qt[:HEAD_DIM], qf[:8], zeros56], axis=0).astype(BF16)
    qtb = jnp.concatenate([qf[8:], zeros56, qt[HEAD_DIM:]], axis=0).astype(BF16)

    def block(j, carry, masked):
        ma, la, acca, mb, lb, accb = carry
        rows = pl.ds(pl.multiple_of(j * nb, nb), nb)
        vt = vt_ref[j]
        if masked:
            kidx = lax.broadcasted_iota(jnp.int32, (nb, nb), 0)
            qidx = lax.broadcasted_iota(jnp.int32, (nb, nb), 1)
            keep = kidx <= qidx
        out = []
        for k_aug, qt_aug, vth, m, l, acc in (
                (ka_ref[rows, :], qta, vt[:HEAD_DIM], ma, la, acca),
                (kb_ref[rows, :], qtb, vt[HEAD_DIM:], mb, lb, accb)):
            st = _dot(k_aug, qt_aug)
            if masked:
                st = jnp.where(keep, st, NEG)
            m_new = jnp.maximum(m, jnp.max(st, axis=0, keepdims=True))
            alpha = jnp.exp(m - m_new)
            p = jnp.exp(st - m_new)
            l_new = alpha * l + jnp.sum(p, axis=0, keepdims=True)
            acc_new = alpha * acc + _dot(vth, p.astype(BF16))
            out += [m_new, l_new, acc_new]
        return tuple(out)

    init = (jnp.full((1, nb), NEG, F32), jnp.zeros((1, nb), F32), jnp.zeros((HEAD_DIM, nb), F32)) * 2
    carry = lax.fori_loop(0, qi, functools.partial(block, masked=False), init)
    _, la, acca, _, lb, accb = block(qi, carry, masked=True)
    ot = jnp.concatenate([acca * (1.0 / la), accb * (1.0 / lb)], axis=0)
    o_ref[...] = (ot.T * _silu(z_ref[...].astype(F32))).astype(BF16)


def _fox_attention(proj, kfeat, qfeat_t, seq):
    t = proj.shape[0]
    b = t // seq
    nq = seq // FOX_BLK
    cb = lambda off: off // LANES
    return pl.pallas_call(
        _fox_kernel,
        grid=(b, N_PAIRS, nq),
        in_specs=[
            pl.BlockSpec((FOX_BLK, LANES), lambda bi, p, qi: (bi * nq + qi, cb(COL_C_Q) + p)),
            pl.BlockSpec((seq, LANES), lambda bi, p, qi: (bi, cb(COL_C_K) + p)),
            pl.BlockSpec((seq, LANES), lambda bi, p, qi: (bi, cb(COL_C_V) + p)),
            pl.BlockSpec((seq, LANES), lambda bi, p, qi: (bi, p)),
            pl.BlockSpec((None, FOX_BLK // CHUNK, 16, CHUNK), lambda bi, p, qi: (bi, qi, p, 0)),
            pl.BlockSpec((FOX_BLK, LANES), lambda bi, p, qi: (bi * nq + qi, cb(COL_C_Z) + p)),
        ],
        out_specs=pl.BlockSpec((FOX_BLK, LANES), lambda bi, p, qi: (bi * nq + qi, p)),
        out_shape=jax.ShapeDtypeStruct((t, D_MODEL), BF16),
        scratch_shapes=[
            pltpu.VMEM((seq, LANES), BF16),
            pltpu.VMEM((seq, LANES), BF16),
            pltpu.VMEM((seq // FOX_BLK, LANES, FOX_BLK), BF16),
        ],
        compiler_params=_cparams(3),
        name="fox_attn",
    )(proj, proj, proj, kfeat, qfeat_t, proj)


def _rope(x, cos, sin_signed, first_half):
    partner = jnp.where(first_half, pltpu.roll(x, 96, 1), pltpu.roll(x, 32, 1))
    return x * cos + partner * sin_signed


def _swa_kernel(sinks_ref, q_ref, kc_ref, kp_ref, vc_ref, vp_ref, z_ref, cosc_ref, sinc_ref, cosp_ref, sinp_ref,
                o_ref):
    i = pl.program_id(1)
    blk = CHUNK
    lane = lax.broadcasted_iota(jnp.int32, (blk, LANES), 1)
    first_half = (lane % HEAD_DIM) < (HEAD_DIM // 2)
    even_head = lane < HEAD_DIM
    cosc, sinc, cosp, sinp = cosc_ref[...], sinc_ref[...], cosp_ref[...], sinp_ref[...]

    qidx = lax.broadcasted_iota(jnp.int32, (blk, 2 * blk), 0)
    sidx = lax.broadcasted_iota(jnp.int32, (blk, 2 * blk), 1)
    first_key = jnp.where(i > 0, 0, blk)
    keep = (sidx > qidx) & (sidx <= qidx + blk) & (sidx >= first_key)

    for kv in range(SWA_KV_HEADS):
        cols = slice(kv * LANES, (kv + 1) * LANES)
        k_prev = _rope(kp_ref[:, cols].astype(F32), cosp, sinp, first_half)
        k_cur = _rope(kc_ref[:, cols].astype(F32), cosc, sinc, first_half)
        k_all = jnp.concatenate([k_prev, k_cur], axis=0).astype(BF16)
        v_all = jnp.concatenate([vp_ref[:, cols], vc_ref[:, cols]], axis=0)
        for pp in range(2):
            pair = 2 * kv + pp
            pcols = slice(pair * LANES, (pair + 1) * LANES)
            qr = _rope(q_ref[:, pcols].astype(F32), cosc, sinc, first_half)
            outs = []
            for e in range(2):
                sel = even_head if e == 0 else jnp.logical_not(even_head)
                qh = jnp.where(sel, qr, 0.0).astype(BF16)
                sc = jnp.where(keep, _dot_nt(qh, k_all), NEG)
                sink = sinks_ref[2 * pair + e]
                m = jnp.maximum(jnp.max(sc, axis=-1, keepdims=True), sink)
                p = jnp.exp(sc - m)
                denom = jnp.sum(p, axis=-1, keepdims=True) + jnp.exp(sink - m)
                outs.append(_dot(p.astype(BF16), v_all) * (1.0 / denom))
            y = jnp.where(even_head, outs[0], outs[1])
            o_ref[:, pcols] = (y * _silu(z_ref[:, pcols].astype(F32))).astype(BF16)


def _swa_attention(proj, sinks, cos_t, sin_t, seq):
    t = proj.shape[0]
    b = t // seq
    nblk = seq // CHUNK
    kvw = SWA_KV_HEADS * LANES
    cur = lambda w, off: pl.BlockSpec((CHUNK, w), lambda bi, i: (bi * nblk + i, off // w))
    prev = lambda w, off: pl.BlockSpec((CHUNK, w), lambda bi, i: (bi * nblk + jnp.maximum(i - 1, 0), off // w))
    return pl.pallas_call(
        _swa_kernel,
        grid=(b, nblk),
        in_specs=[
            pl.BlockSpec(memory_space=pltpu.SMEM),
            cur(D_MODEL, COL_B_Q),
            cur(kvw, COL_B_K), prev(kvw, COL_B_K),
            cur(kvw, COL_B_V), prev(kvw, COL_B_V),
            cur(D_MODEL, COL_B_Z),
            pl.BlockSpec((CHUNK, LANES), lambda bi, i: (i, 0)),
            pl.BlockSpec((CHUNK, LANES), lambda bi, i: (i, 0)),
            pl.BlockSpec((CHUNK, LANES), lambda bi, i: (jnp.maximum(i - 1, 0), 0)),
            pl.BlockSpec((CHUNK, LANES), lambda bi, i: (jnp.maximum(i - 1, 0), 0)),
        ],
        out_specs=pl.BlockSpec((CHUNK, D_MODEL), lambda bi, i: (bi * nblk + i, 0)),
        out_shape=jax.ShapeDtypeStruct((t, D_MODEL), BF16),
        compiler_params=_cparams(2),
        name="swa_attn",
    )(sinks, proj, proj, proj, proj, proj, proj, cos_t, sin_t, cos_t, sin_t)


HALO = 8


def _mamba_kernel(xbc_ref, z_ref, sm_ref, cw_ref, cb_ref, dtb_ref, alog_ref, dsk_ref, nw_ref, o_ref,
                  ext_ref, state_ref):
    c = pl.program_id(1)
    n = CHUNK

    @pl.when(c == 0)
    def _():
        ext_ref[0:HALO, :] = jnp.zeros((HALO, ext_ref.shape[1]), F32)
        state_ref[...] = jnp.zeros_like(state_ref)

    ext_ref[HALO:HALO + n, :] = xbc_ref[...].astype(F32)
    conv = cb_ref[...] + cw_ref[0:1, :] * ext_ref[HALO - 3:HALO - 3 + n, :]
    for k in range(1, CONV_WIDTH):
        conv = conv + cw_ref[k:k + 1, :] * ext_ref[HALO - 3 + k:HALO - 3 + k + n, :]
    tail = ext_ref[n:n + HALO, :]
    ext_ref[0:HALO, :] = tail
    xbc = _silu(conv)
    xs = xbc[:, :D_MODEL]
    bm = xbc[:, D_MODEL:D_MODEL + SSM_GROUPS * SSM_STATE]
    cm = xbc[:, D_MODEL + SSM_GROUPS * SSM_STATE:]
    xs_b = xs.astype(BF16)

    dt = _softplus(sm_ref[...] + dtb_ref[...])
    a_dt = dt * (-jnp.exp(alog_ref[...]))
    row = lax.broadcasted_iota(jnp.int32, (n, n), 0)
    col = lax.broadcasted_iota(jnp.int32, (n, n), 1)
    causal = row >= col
    tri = causal.astype(BF16)
    loc = _dot(tri, jnp.concatenate(_split3(a_dt), axis=1))
    acum = loc[:, :LANES] + loc[:, LANES:2 * LANES] + loc[:, 2 * LANES:]
    acum_t = acum.T
    dt_t = dt.T
    eacum = jnp.exp(acum)
    lane = lax.broadcasted_iota(jnp.int32, (n, LANES), 1)
    even_head = lane < HEAD_DIM

    ys = []
    for g in range(SSM_GROUPS):
        gcols = slice(g * SSM_STATE, (g + 1) * SSM_STATE)
        bg = bm[:, gcols]
        cg = cm[:, gcols]
        cbm = _dot_nt(cg.astype(BF16), bg.astype(BF16))
        bg_t = bg.T
        for pp in range(2):
            pair = 2 * g + pp
            pcols = slice(pair * LANES, (pair + 1) * LANES)
            st_old = state_ref[:, pcols]
            rhs = jnp.concatenate([xs_b[:, pcols], st_old.astype(BF16)], axis=0)
            outs, news, sdec = [], [], []
            for e in range(2):
                h = 2 * pair + e
                a_col = acum[:, h:h + 1]
                a_row = acum_t[h:h + 1, :]
                dt_row = dt_t[h:h + 1, :]
                decay = jnp.exp(jnp.where(causal, a_col - a_row, NEG))
                m_in = (cbm * decay * dt_row).astype(BF16)
                c_off = (cg * eacum[:, h:h + 1]).astype(BF16)
                outs.append(_dot(jnp.concatenate([m_in, c_off], axis=1), rhs))
                a_last = a_row[:, n - 1:n]
                w_row = jnp.exp(a_last - a_row) * dt_row
                news.append(_dot((bg_t * w_row).astype(BF16), xs_b[:, pcols]))
                sdec.append(jnp.exp(a_last))
            ys.append(jnp.where(even_head, outs[0], outs[1]))
            state_ref[:, pcols] = (st_old * jnp.where(even_head, sdec[0], sdec[1])
                                   + jnp.where(even_head, news[0], news[1]))

    y = jnp.concatenate(ys, axis=1) + dsk_ref[...] * xs
    y = y * _silu(z_ref[...].astype(F32))
    gw = D_MODEL // SSM_GROUPS
    for g in range(SSM_GROUPS):
        gc = slice(g * gw, (g + 1) * gw)
        yg = y[:, gc]
        ms = jnp.mean(yg * yg, axis=-1, keepdims=True)
        o_ref[:, gc] = (yg * lax.rsqrt(ms + NORM_EPS) * nw_ref[:, gc]).astype(BF16)


def _mamba(proj, small, conv_w, conv_b, dtb_row, alog_row, dskip_full, norm_w, seq):
    t = proj.shape[0]
    b = t // seq
    nchunk = seq // CHUNK
    xw = 2 * D_MODEL
    const = lambda shape: pl.BlockSpec(shape, lambda bi, c: (0,) * len(shape))
    return pl.pallas_call(
        _mamba_kernel,
        grid=(b, nchunk),
        in_specs=[
            pl.BlockSpec((CHUNK, xw), lambda bi, c: (bi * nchunk + c, COL_A_XBC // xw)),
            pl.BlockSpec((CHUNK, D_MODEL), lambda bi, c: (bi * nchunk + c, COL_A_Z // D_MODEL)),
            pl.BlockSpec((CHUNK, LANES), lambda bi, c: (bi * nchunk + c, 0)),
            const((CONV_WIDTH, xw)), const((1, xw)), const((1, LANES)), const((1, LANES)),
            const((1, D_MODEL)), const((1, D_MODEL)),
        ],
        out_specs=pl.BlockSpec((CHUNK, D_MODEL), lambda bi, c: (bi * nchunk + c, 0)),
        out_shape=jax.ShapeDtypeStruct((t, D_MODEL), BF16),
        scratch_shapes=[
            pltpu.VMEM((CHUNK + HALO, xw), F32),
            pltpu.VMEM((SSM_STATE, D_MODEL), F32),
        ],
        compiler_params=_cparams(2),
        name="mamba",
    )(proj, proj, small, conv_w, conv_b, dtb_row, alog_row, dskip_full, norm_w)


OUT_TM = 512


def _merge_kernel(x_ref, ya_ref, yb_ref, yc_ref, g_ref, gb_ref, wp_ref, wo_ref, fnw_ref, o_ref, *, final_norm):
    merged = None
    for i, y_ref in enumerate((ya_ref, yb_ref, yc_ref)):
        branch = _dot(y_ref[...], wp_ref[i])
        gcols = slice(i * D_MODEL, (i + 1) * D_MODEL)
        gate = 1.0 / (1.0 + jnp.exp(-(g_ref[:, gcols].astype(F32) + gb_ref[:, gcols])))
        merged = gate * branch if merged is None else merged + gate * branch
    x = x_ref[...] + _dot(merged.astype(BF16), wo_ref[...])
    if final_norm:
        ms = jnp.mean(x * x, axis=-1, keepdims=True)
        x = x * lax.rsqrt(ms + NORM_EPS) * fnw_ref[...]
    o_ref[...] = x


def _merge(x2, ya, yb, yc, proj, gate_bias, w_proj, w_out, final_norm_w, final_norm):
    t = x2.shape[0]
    gw = 3 * D_MODEL
    row = lambda w, cbi: pl.BlockSpec((OUT_TM, w), lambda i: (i, cbi))
    const = lambda shape: pl.BlockSpec(shape, lambda i: (0,) * len(shape))
    return pl.pallas_call(
        functools.partial(_merge_kernel, final_norm=final_norm),
        grid=(t // OUT_TM,),
        in_specs=[
            row(D_MODEL, 0), row(D_MODEL, 0), row(D_MODEL, 0), row(D_MODEL, 0),
            row(gw, COL_GATES // gw),
            const((1, gw)), const((3, D_MODEL, D_MODEL)), const((D_MODEL, D_MODEL)), const((1, D_MODEL)),
        ],
        out_specs=row(D_MODEL, 0),
        out_shape=jax.ShapeDtypeStruct((t, D_MODEL), F32),
        compiler_params=_cparams(1),
        name="merge_out",
    )(x2, ya, yb, yc, proj, gate_bias, w_proj, w_out, final_norm_w)


def _dup_heads(w):
    d = w.shape[0]
    w4 = w.reshape(d, SWA_KV_HEADS, HEAD_DIM)
    return jnp.concatenate([w4, w4], axis=-1).reshape(d, SWA_KV_HEADS * LANES)


def _layout_w_in(w):
    offs = np.cumsum((0,) + _IN_SIZES)
    (a_xbc, a_z, a_dt, b_q, b_k, b_v, b_z, c_q, c_k, c_v, c_f, c_z, gates) = [
        w[:, offs[i]:offs[i + 1]] for i in range(len(_IN_SIZES))]
    scale = HEAD_DIM ** -0.5
    big = jnp.concatenate([a_xbc, a_z, b_z, c_z, b_q * scale, c_q * scale, c_k, c_v, gates,
                           _dup_heads(b_k), _dup_heads(b_v)], axis=1).astype(BF16)
    small = jnp.concatenate([a_dt, c_f, jnp.zeros((w.shape[0], LANES - 32), w.dtype)], axis=1)
    s_hi = small.astype(BF16)
    s_lo = (small - s_hi.astype(F32)).astype(BF16)
    return big, jnp.stack([s_hi, s_lo])


def _lane_row(v, start):
    return jnp.zeros((1, LANES), F32).at[0, start:start + N_HEADS].set(v.astype(F32))


def _rope_tables(seq):
    pos = jnp.arange(seq, dtype=F32)
    inv_freq = ROPE_THETA ** (-jnp.arange(0, HEAD_DIM, 2, dtype=F32) / HEAD_DIM)
    ang = pos[:, None] * inv_freq[None, :]
    cos, sin = jnp.cos(ang), jnp.sin(ang)
    cos_t = jnp.concatenate([cos, cos, cos, cos], axis=1)
    sin_t = jnp.concatenate([-sin, sin, -sin, sin], axis=1)
    return cos_t, sin_t


def kernel(x, norm_w, w_in, conv_w, conv_b, dt_bias, a_log, d_skip, ssm_norm_w,
           sinks, f_bias, gate_bias, w_proj, w_out, final_norm_w):
    b, s, d = x.shape
    depth = norm_w.shape[0]
    assert d == D_MODEL and s % FOX_BLK == 0 and (b * s) % IN_TM == 0
    cos_t, sin_t = _rope_tables(s)
    x2 = x.reshape(b * s, d)
    for layer in range(depth):
        w_big, w_small = _layout_w_in(w_in[layer])
        proj, small = _inproj(x2, norm_w[layer][None, :], w_big, w_small)
        kfeat, qfeat_t = _fox_scan(small, _lane_row(f_bias[layer], LANE_F), s)
        ya = _mamba(proj, small, conv_w[layer], conv_b[layer][None, :], _lane_row(dt_bias[layer], 0),
                    _lane_row(a_log[layer], 0), jnp.repeat(d_skip[layer], HEAD_DIM)[None, :],
                    ssm_norm_w[layer][None, :], s)
        yb = _swa_attention(proj, sinks[layer], cos_t, sin_t, s)
        yc = _fox_attention(proj, kfeat, qfeat_t, s)
        x2 = _merge(x2, ya, yb, yc, proj, gate_bias[layer].reshape(1, 3 * d),
                    w_proj[layer].astype(BF16), w_out[layer].astype(BF16), final_norm_w[None, :],
                    final_norm=(layer == depth - 1))
    return x2.reshape(b, s, d)
```

```python
import functools
import math

import jax
import jax.numpy as jnp
import numpy as np
from jax import lax
from jax.experimental import pallas as pl
from jax.experimental.pallas import tpu as pltpu

F32 = jnp.float32
BF16 = jnp.bfloat16

D_MODEL = 1024
HEAD_DIM = 64
N_HEADS = 16
N_PAIRS = N_HEADS // 2
LANES = 128
SSM_GROUPS = 4
SSM_STATE = 128
CONV_WIDTH = 4
CHUNK = 128
SWA_KV_HEADS = 4
ROPE_THETA = 10000.0
NORM_EPS = 1e-6
NEG = -1e30
LOG2E = math.log2(math.e)

COL_A_XBC = 0
COL_A_Z = 2048
COL_B_Z = 3072
COL_C_Z = 4096
COL_B_Q = 5120
COL_C_Q = 6144
COL_C_K = 7168
COL_C_V = 8192
COL_GATES = 9216
COL_B_K = 12288
COL_B_V = 12800
N_PROJ = 13312

_IN_SIZES = (2048, 1024, 16, 1024, 256, 256, 1024, 1024, 1024, 1024, 16, 1024, 3072)

VMEM_LIMIT = 56 * 1024 * 1024


def _cparams(n_axes):
    return pltpu.CompilerParams(dimension_semantics=("arbitrary",) * n_axes, vmem_limit_bytes=VMEM_LIMIT)


def _split3(x):
    hi = x.astype(BF16)
    r1 = x - hi.astype(F32)
    mid = r1.astype(BF16)
    lo = (r1 - mid.astype(F32)).astype(BF16)
    return hi, mid, lo


def _dot(a, b):
    return jnp.dot(a, b, preferred_element_type=F32)


def _dot_nt(a, b):
    return lax.dot_general(a, b, (((1,), (1,)), ((), ())), preferred_element_type=F32)


def _silu(x):
    return x * (1.0 / (1.0 + jnp.exp(-x)))


def _softplus(x):
    return jnp.maximum(x, 0.0) + jnp.log1p(jnp.exp(-jnp.abs(x)))


IN_TM = 1024
IN_TN = 1024


def _inproj_kernel(x_ref, nw_ref, w_ref, wsm_ref, o_ref, osm_ref, h_ref):
    @pl.when(pl.program_id(1) == 0)
    def _():
        x = x_ref[...]
        ms = jnp.mean(x * x, axis=-1, keepdims=True)
        h = x * lax.rsqrt(ms + NORM_EPS) * nw_ref[...]
        hh = h.astype(BF16)
        h_ref[...] = hh
        hl = (h - hh.astype(F32)).astype(BF16)
        osm_ref[...] = _dot(hh, wsm_ref[0]) + _dot(hl, wsm_ref[0]) + _dot(hh, wsm_ref[1])

    o_ref[...] = _dot(h_ref[...], w_ref[...]).astype(BF16)


def _inproj(x2, norm_w, w_big, w_small):
    t = x2.shape[0]
    grid = (t // IN_TM, N_PROJ // IN_TN)
    return pl.pallas_call(
        _inproj_kernel,
        grid=grid,
        in_specs=[
            pl.BlockSpec((IN_TM, D_MODEL), lambda i, j: (i, 0)),
            pl.BlockSpec((1, D_MODEL), lambda i, j: (0, 0)),
            pl.BlockSpec((D_MODEL, IN_TN), lambda i, j: (0, j)),
            pl.BlockSpec((2, D_MODEL, LANES), lambda i, j: (0, 0, 0)),
        ],
        out_specs=[
            pl.BlockSpec((IN_TM, IN_TN), lambda i, j: (i, j)),
            pl.BlockSpec((IN_TM, LANES), lambda i, j: (i, 0)),
        ],
        out_shape=[
            jax.ShapeDtypeStruct((t, N_PROJ), BF16),
            jax.ShapeDtypeStruct((t, LANES), F32),
        ],
        scratch_shapes=[pltpu.VMEM((IN_TM, D_MODEL), BF16)],
        compiler_params=_cparams(2),
        name="inproj",
    )(x2, norm_w, w_big, w_small)


LANE_F = 16


def _scan_selectors():
    selk = np.zeros((3 * LANES, N_PAIRS * LANES), np.float32)
    konst = np.zeros((1, N_PAIRS * LANES), np.float32)
    selq = np.zeros((LANES, 3 * LANES), np.float32)
    qonst = np.zeros((LANES, 1), np.float32)
    for p in range(N_PAIRS):
        for e in range(2):
            src = LANE_F + 2 * p + e
            kbase = p * LANES + (64 if e == 0 else 0)
            qbase = 16 * p + 8 * e
            for t in range(3):
                konst[0, kbase + t] = 1.0
                selk[t * LANES + src, kbase + 3 + t] = 1.0
                selq[qbase + t, t * LANES + src] = 1.0
                qonst[qbase + 3 + t, 0] = -1.0
    return selk, konst, selq, qonst


def _scan_kernel(sm_ref, fb_ref, selk_ref, konst_ref, selq_ref, qonst_ref, kf_ref, qf_ref):
    s = sm_ref.shape[0]
    row = lax.broadcasted_iota(jnp.int32, (CHUNK, CHUNK), 0)
    col = lax.broadcasted_iota(jnp.int32, (CHUNK, CHUNK), 1)
    tri = (row >= col).astype(BF16)
    fb = fb_ref[...]

    def body(c, carry):
        rows = pl.ds(pl.multiple_of(c * CHUNK, CHUNK), CHUNK)
        v = sm_ref[rows, :] + fb
        lf = (jnp.minimum(v, 0.0) - jnp.log1p(jnp.exp(-jnp.abs(v)))) * LOG2E
        hi, mid, lo = _split3(lf)
        local = _dot(tri, jnp.concatenate([hi, mid, lo], axis=1))
        cum = local[:, :LANES] + local[:, LANES:2 * LANES] + local[:, 2 * LANES:] + carry
        c3 = jnp.concatenate(_split3(cum), axis=1)
        kf_ref[rows, :] = (_dot(c3, selk_ref[...]) + konst_ref[...]).astype(BF16)
        qf_ref[c] = _dot_nt(selq_ref[...], c3) + qonst_ref[...]
        return cum[CHUNK - 1:CHUNK, :]

    lax.fori_loop(0, s // CHUNK, body, jnp.zeros((1, LANES), F32))


def _fox_scan(small, fb_row, seq):
    t = small.shape[0]
    b = t // seq
    selk, konst, selq, qonst = _scan_selectors()
    const = lambda shape: pl.BlockSpec(shape, lambda i: (0,) * len(shape))
    return pl.pallas_call(
        _scan_kernel,
        grid=(b,),
        in_specs=[
            pl.BlockSpec((seq, LANES), lambda i: (i, 0)),
            const((1, LANES)),
            const(selk.shape), const(konst.shape), const(selq.shape), const(qonst.shape),
        ],
        out_specs=[
            pl.BlockSpec((seq, N_PAIRS * LANES), lambda i: (i, 0)),
            pl.BlockSpec((None, seq // CHUNK, LANES, CHUNK), lambda i: (i, 0, 0, 0)),
        ],
        out_shape=[
            jax.ShapeDtypeStruct((t, N_PAIRS * LANES), BF16),
            jax.ShapeDtypeStruct((b, seq // CHUNK, LANES, CHUNK), F32),
        ],
        compiler_params=_cparams(1),
        name="fox_scan",
    )(small, fb_row, jnp.asarray(selk, BF16), jnp.asarray(konst, F32), jnp.asarray(selq, BF16),
      jnp.asarray(qonst, F32))


FOX_BLK = 256
FOX_PAIRS = 2
FOX_HEADS = 2 * FOX_PAIRS
FOX_UNROLL = 4
FOX_VROWS = HEAD_DIM + 16


def _fox_kernel(q_ref, k_ref, v_ref, kf_ref, qf_ref, z_ref, o_ref,
                kaug_ref, vt_ref, qt_ref, acc_ref, m_ref, a_ref, mx_ref, st_ref, p_ref):
    qi = pl.program_id(2)
    seq = k_ref.shape[0]
    nb = FOX_BLK

    @pl.when(qi == 0)
    def _prologue():
        lane = lax.broadcasted_iota(jnp.int32, (nb, LANES), 1)
        for pr in range(FOX_PAIRS):
            cols = slice(pr * LANES, (pr + 1) * LANES)
            for c in range(seq // nb):
                rows = slice(c * nb, (c + 1) * nb)
                k2 = k_ref[rows, cols]
                kf = kf_ref[rows, cols]
                kaug_ref[2 * pr, rows, :] = jnp.where(lane < HEAD_DIM, k2, kf)
                kaug_ref[2 * pr + 1, rows, :] = jnp.where(lane >= HEAD_DIM, k2, kf)
                vt = v_ref[rows, cols].astype(F32).T.astype(BF16)
                ones = jnp.ones((FOX_VROWS - HEAD_DIM, nb), BF16)
                vt_ref[2 * pr, c] = jnp.concatenate([vt[:HEAD_DIM], ones], axis=0)
                vt_ref[2 * pr + 1, c] = jnp.concatenate([vt[HEAD_DIM:], ones], axis=0)

    zeros56 = jnp.zeros((HEAD_DIM - 8, nb), F32)
    for pr in range(FOX_PAIRS):
        qt = q_ref[:, pr * LANES:(pr + 1) * LANES].astype(F32).T
        qf = jnp.concatenate([qf_ref[c, 16 * pr:16 * (pr + 1), :] for c in range(nb // CHUNK)], axis=1)
        qt_ref[2 * pr] = jnp.concatenate([qt[:HEAD_DIM], qf[:8], zeros56], axis=0).astype(BF16)
        qt_ref[2 * pr + 1] = jnp.concatenate([qf[8:], zeros56, qt[HEAD_DIM:]], axis=0).astype(BF16)
    acc_ref[...] = jnp.zeros_like(acc_ref)

    def scores(t, slot):
        rows = pl.ds(pl.multiple_of(t * nb, nb), nb)
        for h in range(FOX_HEADS):
            st = _dot(kaug_ref[h, rows, :], qt_ref[h])
            st_ref[slot, h] = st
            mx_ref[slot, h] = jnp.max(st, axis=0, keepdims=True)

    def softmax(slot, key_limit):
        if key_limit is not None:
            kidx = lax.broadcasted_iota(jnp.int32, (nb, nb), 0)
            qidx = lax.broadcasted_iota(jnp.int32, (nb, nb), 1)
            keep = kidx <= qidx + key_limit
        for h in range(FOX_HEADS):
            st = st_ref[slot, h]
            if key_limit is None:
                blk_max = mx_ref[slot, h]
            else:
                st = jnp.where(keep, st, NEG)
                blk_max = jnp.max(st, axis=0, keepdims=True)
            m_old = m_ref[h]
            m_new = jnp.maximum(m_old, blk_max)
            alpha = jnp.exp2(m_old - m_new)
            p = jnp.exp2(st - m_new)
            p_ref[slot, h] = p.astype(BF16)
            m_ref[h] = m_new
            a_ref[slot, h] = alpha

    def values(t, slot):
        for h in range(FOX_HEADS):
            acc_ref[h] = a_ref[slot, h] * acc_ref[h] + _dot(vt_ref[h, t], p_ref[slot, h])

    def step(t, slot):
        values(t - 2, slot)
        softmax(1 - slot, None)
        scores(t, slot)

    m_ref[...] = jnp.full(m_ref.shape, NEG, F32)
    scores(0, 0)
    scores(1, 1)
    softmax(0, jnp.where(qi > 0, nb, 0))

    n_steps = jnp.maximum(qi - 1, 0)
    n_trips = lax.div(n_steps, FOX_UNROLL)

    def trip(g, carry):
        for i in range(FOX_UNROLL):
            step(2 + g * FOX_UNROLL + i, i % 2)
        return carry

    lax.fori_loop(0, n_trips, trip, 0)
    for i in range(FOX_UNROLL - 1):
        @pl.when(i < n_steps - n_trips * FOX_UNROLL)
        def _leftover():
            step(2 + n_trips * FOX_UNROLL + i, i % 2)

    @pl.when(qi > 0)
    def _diagonal():
        values(qi - 1, lax.rem(qi - 1, 2))
        softmax(lax.rem(qi, 2), 0)

    values(qi, lax.rem(qi, 2))
    for pr in range(FOX_PAIRS):
        ot = jnp.concatenate([acc_ref[2 * pr + e, :HEAD_DIM, :] * (1.0 / acc_ref[2 * pr + e, HEAD_DIM:HEAD_DIM + 1, :])
                              for e in range(2)], axis=0)
        cols = slice(pr * LANES, (pr + 1) * LANES)
        o_ref[:, cols] = (ot.T * _silu(z_ref[:, cols].astype(F32))).astype(BF16)


def _fox_attention(proj, kfeat, qfeat_t, seq):
    t = proj.shape[0]
    b = t // seq
    nq = seq // FOX_BLK
    w = FOX_PAIRS * LANES
    cb = lambda off: off // w
    return pl.pallas_call(
        _fox_kernel,
        grid=(b, N_PAIRS // FOX_PAIRS, nq),
        in_specs=[
            pl.BlockSpec((FOX_BLK, w), lambda bi, p, qi: (bi * nq + qi, cb(COL_C_Q) + p)),
            pl.BlockSpec((seq, w), lambda bi, p, qi: (bi, cb(COL_C_K) + p)),
            pl.BlockSpec((seq, w), lambda bi, p, qi: (bi, cb(COL_C_V) + p)),
            pl.BlockSpec((seq, w), lambda bi, p, qi: (bi, p)),
            pl.BlockSpec((None, FOX_BLK // CHUNK, 16 * FOX_PAIRS, CHUNK), lambda bi, p, qi: (bi, qi, p, 0)),
            pl.BlockSpec((FOX_BLK, w), lambda bi, p, qi: (bi * nq + qi, cb(COL_C_Z) + p)),
        ],
        out_specs=pl.BlockSpec((FOX_BLK, w), lambda bi, p, qi: (bi * nq + qi, p)),
        out_shape=jax.ShapeDtypeStruct((t, D_MODEL), BF16),
        scratch_shapes=[
            pltpu.VMEM((FOX_HEADS, seq, LANES), BF16),
            pltpu.VMEM((FOX_HEADS, seq // FOX_BLK, FOX_VROWS, FOX_BLK), BF16),
            pltpu.VMEM((FOX_HEADS, LANES, FOX_BLK), BF16),
            pltpu.VMEM((FOX_HEADS, FOX_VROWS, FOX_BLK), F32),
            pltpu.VMEM((FOX_HEADS, 1, FOX_BLK), F32),
            pltpu.VMEM((2, FOX_HEADS, 1, FOX_BLK), F32),
            pltpu.VMEM((2, FOX_HEADS, 1, FOX_BLK), F32),
            pltpu.VMEM((2, FOX_HEADS, FOX_BLK, FOX_BLK), F32),
            pltpu.VMEM((2, FOX_HEADS, FOX_BLK, FOX_BLK), BF16),
        ],
        compiler_params=_cparams(3),
        name="fox_attn",
    )(proj, proj, proj, kfeat, qfeat_t, proj)


def _rope(x, cos, sin_signed, first_half):
    partner = jnp.where(first_half, pltpu.roll(x, 96, 1), pltpu.roll(x, 32, 1))
    return x * cos + partner * sin_signed


def _swa_kernel(sinks_ref, q_ref, kc_ref, kp_ref, vc_ref, vp_ref, z_ref, cosc_ref, sinc_ref, cosp_ref, sinp_ref,
                o_ref):
    i = pl.program_id(1)
    blk = CHUNK
    lane = lax.broadcasted_iota(jnp.int32, (blk, LANES), 1)
    first_half = (lane % HEAD_DIM) < (HEAD_DIM // 2)
    even_head = lane < HEAD_DIM
    cosc, sinc, cosp, sinp = cosc_ref[...], sinc_ref[...], cosp_ref[...], sinp_ref[...]

    qidx = lax.broadcasted_iota(jnp.int32, (blk, 2 * blk), 0)
    sidx = lax.broadcasted_iota(jnp.int32, (blk, 2 * blk), 1)
    first_key = jnp.where(i > 0, 0, blk)
    keep = (sidx > qidx) & (sidx <= qidx + blk) & (sidx >= first_key)

    for kv in range(SWA_KV_HEADS):
        cols = slice(kv * LANES, (kv + 1) * LANES)
        k_prev = _rope(kp_ref[:, cols].astype(F32), cosp, sinp, first_half)
        k_cur = _rope(kc_ref[:, cols].astype(F32), cosc, sinc, first_half)
        k_all = jnp.concatenate([k_prev, k_cur], axis=0).astype(BF16)
        v_all = jnp.concatenate([vp_ref[:, cols], vc_ref[:, cols]], axis=0)
        for pp in range(2):
            pair = 2 * kv + pp
            pcols = slice(pair * LANES, (pair + 1) * LANES)
            qr = _rope(q_ref[:, pcols].astype(F32), cosc, sinc, first_half)
            outs = []
            for e in range(2):
                sel = even_head if e == 0 else jnp.logical_not(even_head)
                qh = jnp.where(sel, qr, 0.0).astype(BF16)
                sc = jnp.where(keep, _dot_nt(qh, k_all), NEG)
                sink = sinks_ref[2 * pair + e]
                m = jnp.maximum(jnp.max(sc, axis=-1, keepdims=True), sink)
                p = jnp.exp(sc - m)
                denom = jnp.sum(p, axis=-1, keepdims=True) + jnp.exp(sink - m)
                outs.append(_dot(p.astype(BF16), v_all) * (1.0 / denom))
            y = jnp.where(even_head, outs[0], outs[1])
            o_ref[:, pcols] = (y * _silu(z_ref[:, pcols].astype(F32))).astype(BF16)


def _swa_attention(proj, sinks, cos_t, sin_t, seq):
    t = proj.shape[0]
    b = t // seq
    nblk = seq // CHUNK
    kvw = SWA_KV_HEADS * LANES
    cur = lambda w, off: pl.BlockSpec((CHUNK, w), lambda bi, i: (bi * nblk + i, off // w))
    prev = lambda w, off: pl.BlockSpec((CHUNK, w), lambda bi, i: (bi * nblk + jnp.maximum(i - 1, 0), off // w))
    return pl.pallas_call(
        _swa_kernel,
        grid=(b, nblk),
        in_specs=[
            pl.BlockSpec(memory_space=pltpu.SMEM),
            cur(D_MODEL, COL_B_Q),
            cur(kvw, COL_B_K), prev(kvw, COL_B_K),
            cur(kvw, COL_B_V), prev(kvw, COL_B_V),
            cur(D_MODEL, COL_B_Z),
            pl.BlockSpec((CHUNK, LANES), lambda bi, i: (i, 0)),
            pl.BlockSpec((CHUNK, LANES), lambda bi, i: (i, 0)),
            pl.BlockSpec((CHUNK, LANES), lambda bi, i: (jnp.maximum(i - 1, 0), 0)),
            pl.BlockSpec((CHUNK, LANES), lambda bi, i: (jnp.maximum(i - 1, 0), 0)),
        ],
        out_specs=pl.BlockSpec((CHUNK, D_MODEL), lambda bi, i: (bi * nblk + i, 0)),
        out_shape=jax.ShapeDtypeStruct((t, D_MODEL), BF16),
        compiler_params=_cparams(2),
        name="swa_attn",
    )(sinks, proj, proj, proj, proj, proj, proj, cos_t, sin_t, cos_t, sin_t)


HALO = 8


def _mamba_kernel(xbc_ref, z_ref, sm_ref, cw_ref, cb_ref, dtb_ref, alog_ref, dsk_ref, nw_ref, o_ref,
                  ext_ref, state_ref):
    c = pl.program_id(1)
    n = CHUNK

    @pl.when(c == 0)
    def _():
        ext_ref[0:HALO, :] = jnp.zeros((HALO, ext_ref.shape[1]), F32)
        state_ref[...] = jnp.zeros_like(state_ref)

    ext_ref[HALO:HALO + n, :] = xbc_ref[...].astype(F32)
    conv = cb_ref[...] + cw_ref[0:1, :] * ext_ref[HALO - 3:HALO - 3 + n, :]
    for k in range(1, CONV_WIDTH):
        conv = conv + cw_ref[k:k + 1, :] * ext_ref[HALO - 3 + k:HALO - 3 + k + n, :]
    tail = ext_ref[n:n + HALO, :]
    ext_ref[0:HALO, :] = tail
    xbc = _silu(conv)
    xs = xbc[:, :D_MODEL]
    bm = xbc[:, D_MODEL:D_MODEL + SSM_GROUPS * SSM_STATE]
    cm = xbc[:, D_MODEL + SSM_GROUPS * SSM_STATE:]
    xs_b = xs.astype(BF16)

    dt = _softplus(sm_ref[...] + dtb_ref[...])
    a_dt = dt * (-jnp.exp(alog_ref[...]))
    row = lax.broadcasted_iota(jnp.int32, (n, n), 0)
    col = lax.broadcasted_iota(jnp.int32, (n, n), 1)
    causal = row >= col
    tri = causal.astype(BF16)
    loc = _dot(tri, jnp.concatenate(_split3(a_dt), axis=1))
    acum = loc[:, :LANES] + loc[:, LANES:2 * LANES] + loc[:, 2 * LANES:]
    acum_t = acum.T
    dt_t = dt.T
    eacum = jnp.exp(acum)
    lane = lax.broadcasted_iota(jnp.int32, (n, LANES), 1)
    even_head = lane < HEAD_DIM

    ys = []
    for g in range(SSM_GROUPS):
        gcols = slice(g * SSM_STATE, (g + 1) * SSM_STATE)
        bg = bm[:, gcols]
        cg = cm[:, gcols]
        cbm = _dot_nt(cg.astype(BF16), bg.astype(BF16))
        bg_t = bg.T
        for pp in range(2):
            pair = 2 * g + pp
            pcols = slice(pair * LANES, (pair + 1) * LANES)
            st_old = state_ref[:, pcols]
            rhs = jnp.concatenate([xs_b[:, pcols], st_old.astype(BF16)], axis=0)
            outs, news, sdec = [], [], []
            for e in range(2):
                h = 2 * pair + e
                a_col = acum[:, h:h + 1]
                a_row = acum_t[h:h + 1, :]
                dt_row = dt_t[h:h + 1, :]
                decay = jnp.exp(jnp.where(causal, a_col - a_row, NEG))
                m_in = (cbm * decay * dt_row).astype(BF16)
                c_off = (cg * eacum[:, h:h + 1]).astype(BF16)
                outs.append(_dot(jnp.concatenate([m_in, c_off], axis=1), rhs))
                a_last = a_row[:, n - 1:n]
                w_row = jnp.exp(a_last - a_row) * dt_row
                news.append(_dot((bg_t * w_row).astype(BF16), xs_b[:, pcols]))
                sdec.append(jnp.exp(a_last))
            ys.append(jnp.where(even_head, outs[0], outs[1]))
            state_ref[:, pcols] = (st_old * jnp.where(even_head, sdec[0], sdec[1])
                                   + jnp.where(even_head, news[0], news[1]))

    y = jnp.concatenate(ys, axis=1) + dsk_ref[...] * xs
    y = y * _silu(z_ref[...].astype(F32))
    gw = D_MODEL // SSM_GROUPS
    for g in range(SSM_GROUPS):
        gc = slice(g * gw, (g + 1) * gw)
        yg = y[:, gc]
        ms = jnp.mean(yg * yg, axis=-1, keepdims=True)
        o_ref[:, gc] = (yg * lax.rsqrt(ms + NORM_EPS) * nw_ref[:, gc]).astype(BF16)


def _mamba(proj, small, conv_w, conv_b, dtb_row, alog_row, dskip_full, norm_w, seq):
    t = proj.shape[0]
    b = t // seq
    nchunk = seq // CHUNK
    xw = 2 * D_MODEL
    const = lambda shape: pl.BlockSpec(shape, lambda bi, c: (0,) * len(shape))
    return pl.pallas_call(
        _mamba_kernel,
        grid=(b, nchunk),
        in_specs=[
            pl.BlockSpec((CHUNK, xw), lambda bi, c: (bi * nchunk + c, COL_A_XBC // xw)),
            pl.BlockSpec((CHUNK, D_MODEL), lambda bi, c: (bi * nchunk + c, COL_A_Z // D_MODEL)),
            pl.BlockSpec((CHUNK, LANES), lambda bi, c: (bi * nchunk + c, 0)),
            const((CONV_WIDTH, xw)), const((1, xw)), const((1, LANES)), const((1, LANES)),
            const((1, D_MODEL)), const((1, D_MODEL)),
        ],
        out_specs=pl.BlockSpec((CHUNK, D_MODEL), lambda bi, c: (bi * nchunk + c, 0)),
        out_shape=jax.ShapeDtypeStruct((t, D_MODEL), BF16),
        scratch_shapes=[
            pltpu.VMEM((CHUNK + HALO, xw), F32),
            pltpu.VMEM((SSM_STATE, D_MODEL), F32),
        ],
        compiler_params=_cparams(2),
        name="mamba",
    )(proj, proj, small, conv_w, conv_b, dtb_row, alog_row, dskip_full, norm_w)


OUT_TM = 512


def _merge_kernel(x_ref, ya_ref, yb_ref, yc_ref, g_ref, gb_ref, wp_ref, wo_ref, fnw_ref, o_ref, *, final_norm):
    merged = None
    for i, y_ref in enumerate((ya_ref, yb_ref, yc_ref)):
        branch = _dot(y_ref[...], wp_ref[i])
        gcols = slice(i * D_MODEL, (i + 1) * D_MODEL)
        gate = 1.0 / (1.0 + jnp.exp(-(g_ref[:, gcols].astype(F32) + gb_ref[:, gcols])))
        merged = gate * branch if merged is None else merged + gate * branch
    x = x_ref[...] + _dot(merged.astype(BF16), wo_ref[...])
    if final_norm:
        ms = jnp.mean(x * x, axis=-1, keepdims=True)
        x = x * lax.rsqrt(ms + NORM_EPS) * fnw_ref[...]
    o_ref[...] = x


def _merge(x2, ya, yb, yc, proj, gate_bias, w_proj, w_out, final_norm_w, final_norm):
    t = x2.shape[0]
    gw = 3 * D_MODEL
    row = lambda w, cbi: pl.BlockSpec((OUT_TM, w), lambda i: (i, cbi))
    const = lambda shape: pl.BlockSpec(shape, lambda i: (0,) * len(shape))
    return pl.pallas_call(
        functools.partial(_merge_kernel, final_norm=final_norm),
        grid=(t // OUT_TM,),
        in_specs=[
            row(D_MODEL, 0), row(D_MODEL, 0), row(D_MODEL, 0), row(D_MODEL, 0),
            row(gw, COL_GATES // gw),
            const((1, gw)), const((3, D_MODEL, D_MODEL)), const((D_MODEL, D_MODEL)), const((1, D_MODEL)),
        ],
        out_specs=row(D_MODEL, 0),
        out_shape=jax.ShapeDtypeStruct((t, D_MODEL), F32),
        compiler_params=_cparams(1),
        name="merge_out",
    )(x2, ya, yb, yc, proj, gate_bias, w_proj, w_out, final_norm_w)


def _dup_heads(w):
    d = w.shape[0]
    w4 = w.reshape(d, SWA_KV_HEADS, HEAD_DIM)
    return jnp.concatenate([w4, w4], axis=-1).reshape(d, SWA_KV_HEADS * LANES)


def _layout_w_in(w):
    offs = np.cumsum((0,) + _IN_SIZES)
    (a_xbc, a_z, a_dt, b_q, b_k, b_v, b_z, c_q, c_k, c_v, c_f, c_z, gates) = [
        w[:, offs[i]:offs[i + 1]] for i in range(len(_IN_SIZES))]
    scale = HEAD_DIM ** -0.5
    big = jnp.concatenate([a_xbc, a_z, b_z, c_z, b_q * scale, c_q * (scale * LOG2E), c_k, c_v, gates,
                           _dup_heads(b_k), _dup_heads(b_v)], axis=1).astype(BF16)
    small = jnp.concatenate([a_dt, c_f, jnp.zeros((w.shape[0], LANES - 32), w.dtype)], axis=1)
    s_hi = small.astype(BF16)
    s_lo = (small - s_hi.astype(F32)).astype(BF16)
    return big, jnp.stack([s_hi, s_lo])


def _lane_row(v, start):
    return jnp.zeros((1, LANES), F32).at[0, start:start + N_HEADS].set(v.astype(F32))


def _rope_tables(seq):
    pos = jnp.arange(seq, dtype=F32)
    inv_freq = ROPE_THETA ** (-jnp.arange(0, HEAD_DIM, 2, dtype=F32) / HEAD_DIM)
    ang = pos[:, None] * inv_freq[None, :]
    cos, sin = jnp.cos(ang), jnp.sin(ang)
    cos_t = jnp.concatenate([cos, cos, cos, cos], axis=1)
    sin_t = jnp.concatenate([-sin, sin, -sin, sin], axis=1)
    return cos_t, sin_t


def kernel(x, norm_w, w_in, conv_w, conv_b, dt_bias, a_log, d_skip, ssm_norm_w,
           sinks, f_bias, gate_bias, w_proj, w_out, final_norm_w):
    b, s, d = x.shape
    depth = norm_w.shape[0]
    assert d == D_MODEL and s % FOX_BLK == 0 and s >= 2 * FOX_BLK and (b * s) % IN_TM == 0
    cos_t, sin_t = _rope_tables(s)
    x2 = x.reshape(b * s, d)
    for layer in range(depth):
        w_big, w_small = _layout_w_in(w_in[layer])
        proj, small = _inproj(x2, norm_w[layer][None, :], w_big, w_small)
        kfeat, qfeat_t = _fox_scan(small, _lane_row(f_bias[layer], LANE_F), s)
        ya = _mamba(proj, small, conv_w[layer], conv_b[layer][None, :], _lane_row(dt_bias[layer], 0),
                    _lane_row(a_log[layer], 0), jnp.repeat(d_skip[layer], HEAD_DIM)[None, :],
                    ssm_norm_w[layer][None, :], s)
        yb = _swa_attention(proj, sinks[layer], cos_t, sin_t, s)
        yc = _fox_attention(proj, kfeat, qfeat_t, s)
        x2 = _merge(x2, ya, yb, yc, proj, gate_bias[layer].reshape(1, 3 * d),
                    w_proj[layer].astype(BF16), w_out[layer].astype(BF16), final_norm_w[None, :],
                    final_norm=(layer == depth - 1))
    return x2.reshape(b, s, d)
```

```python
import functools
import math

import jax
import jax.numpy as jnp
import numpy as np
from jax import lax
from jax.experimental import pallas as pl
from jax.experimental.pallas import tpu as pltpu

F32 = jnp.float32
BF16 = jnp.bfloat16

D_MODEL = 1024
HEAD_DIM = 64
N_HEADS = 16
N_PAIRS = N_HEADS // 2
LANES = 128
SSM_GROUPS = 4
SSM_STATE = 128
CONV_WIDTH = 4
CHUNK = 128
SWA_KV_HEADS = 4
ROPE_THETA = 10000.0
NORM_EPS = 1e-6
NEG = -1e30
LOG2E = math.log2(math.e)

COL_A_XBC = 0
COL_A_Z = 2048
COL_B_Z = 3072
COL_C_Z = 4096
COL_B_Q = 5120
COL_C_Q = 6144
COL_C_K = 7168
COL_C_V = 8192
COL_GATES = 9216
COL_B_K = 12288
COL_B_V = 12800
N_PROJ = 13312

_IN_SIZES = (2048, 1024, 16, 1024, 256, 256, 1024, 1024, 1024, 1024, 16, 1024, 3072)

VMEM_LIMIT = 56 * 1024 * 1024


def _cparams(n_axes):
    return pltpu.CompilerParams(dimension_semantics=("arbitrary",) * n_axes, vmem_limit_bytes=VMEM_LIMIT)


def _split3(x):
    hi = x.astype(BF16)
    r1 = x - hi.astype(F32)
    mid = r1.astype(BF16)
    lo = (r1 - mid.astype(F32)).astype(BF16)
    return hi, mid, lo


def _dot(a, b):
    return jnp.dot(a, b, preferred_element_type=F32)


def _dot_nt(a, b):
    return lax.dot_general(a, b, (((1,), (1,)), ((), ())), preferred_element_type=F32)


def _silu(x):
    return x * (1.0 / (1.0 + jnp.exp2(x * (-LOG2E))))


def _softplus(x):
    return jnp.maximum(x, 0.0) + jnp.log1p(jnp.exp(-jnp.abs(x)))


IN_TM = 2048
IN_TN = 1024
IN_NORM_ROWS = 512


def _inproj_kernel(x_ref, nw_ref, w_ref, wsm_ref, o_ref, osm_ref, h_ref):
    @pl.when(pl.program_id(1) == 0)
    def _():
        for r in range(IN_TM // IN_NORM_ROWS):
            rows = slice(r * IN_NORM_ROWS, (r + 1) * IN_NORM_ROWS)
            x = x_ref[rows, :]
            ms = jnp.mean(x * x, axis=-1, keepdims=True)
            h = x * lax.rsqrt(ms + NORM_EPS) * nw_ref[...]
            hh = h.astype(BF16)
            h_ref[rows, :] = hh
            hl = (h - hh.astype(F32)).astype(BF16)
            osm_ref[rows, :] = _dot(hh, wsm_ref[0]) + _dot(hl, wsm_ref[0]) + _dot(hh, wsm_ref[1])

    o_ref[...] = _dot(h_ref[...], w_ref[...]).astype(BF16)


def _inproj(x2, norm_w, w_big, w_small):
    t = x2.shape[0]
    grid = (t // IN_TM, N_PROJ // IN_TN)
    return pl.pallas_call(
        _inproj_kernel,
        grid=grid,
        in_specs=[
            pl.BlockSpec((IN_TM, D_MODEL), lambda i, j: (i, 0)),
            pl.BlockSpec((1, D_MODEL), lambda i, j: (0, 0)),
            pl.BlockSpec((D_MODEL, IN_TN), lambda i, j: (0, j)),
            pl.BlockSpec((2, D_MODEL, LANES), lambda i, j: (0, 0, 0)),
        ],
        out_specs=[
            pl.BlockSpec((IN_TM, IN_TN), lambda i, j: (i, j)),
            pl.BlockSpec((IN_TM, LANES), lambda i, j: (i, 0)),
        ],
        out_shape=[
            jax.ShapeDtypeStruct((t, N_PROJ), BF16),
            jax.ShapeDtypeStruct((t, LANES), F32),
        ],
        scratch_shapes=[pltpu.VMEM((IN_TM, D_MODEL), BF16)],
        compiler_params=_cparams(2),
        name="inproj",
    )(x2, norm_w, w_big, w_small)


LANE_F = 16


def _scan_selectors():
    selk = np.zeros((3 * LANES, N_PAIRS * LANES), np.float32)
    konst = np.zeros((1, N_PAIRS * LANES), np.float32)
    selq = np.zeros((LANES, 3 * LANES), np.float32)
    qonst = np.zeros((LANES, 1), np.float32)
    for p in range(N_PAIRS):
        for e in range(2):
            src = LANE_F + 2 * p + e
            kbase = p * LANES + (64 if e == 0 else 0)
            qbase = 16 * p + 8 * e
            for t in range(3):
                konst[0, kbase + t] = 1.0
                selk[t * LANES + src, kbase + 3 + t] = 1.0
                selq[qbase + t, t * LANES + src] = 1.0
                qonst[qbase + 3 + t, 0] = -1.0
    return selk, konst, selq, qonst


def _scan_kernel(sm_ref, fb_ref, selk_ref, konst_ref, selq_ref, qonst_ref, kf_ref, qf_ref):
    s = sm_ref.shape[0]
    row = lax.broadcasted_iota(jnp.int32, (CHUNK, CHUNK), 0)
    col = lax.broadcasted_iota(jnp.int32, (CHUNK, CHUNK), 1)
    tri = (row >= col).astype(BF16)
    fb = fb_ref[...]

    def body(c, carry):
        rows = pl.ds(pl.multiple_of(c * CHUNK, CHUNK), CHUNK)
        v = sm_ref[rows, :] + fb
        lf = (jnp.minimum(v, 0.0) - jnp.log1p(jnp.exp(-jnp.abs(v)))) * LOG2E
        hi, mid, lo = _split3(lf)
        local = _dot(tri, jnp.concatenate([hi, mid, lo], axis=1))
        cum = local[:, :LANES] + local[:, LANES:2 * LANES] + local[:, 2 * LANES:] + carry
        c3 = jnp.concatenate(_split3(cum), axis=1)
        kf_ref[rows, :] = (_dot(c3, selk_ref[...]) + konst_ref[...]).astype(BF16)
        qf_ref[c] = _dot_nt(selq_ref[...], c3) + qonst_ref[...]
        return cum[CHUNK - 1:CHUNK, :]

    lax.fori_loop(0, s // CHUNK, body, jnp.zeros((1, LANES), F32))


def _fox_scan(small, fb_row, seq):
    t = small.shape[0]
    b = t // seq
    selk, konst, selq, qonst = _scan_selectors()
    const = lambda shape: pl.BlockSpec(shape, lambda i: (0,) * len(shape))
    return pl.pallas_call(
        _scan_kernel,
        grid=(b,),
        in_specs=[
            pl.BlockSpec((seq, LANES), lambda i: (i, 0)),
            const((1, LANES)),
            const(selk.shape), const(konst.shape), const(selq.shape), const(qonst.shape),
        ],
        out_specs=[
            pl.BlockSpec((seq, N_PAIRS * LANES), lambda i: (i, 0)),
            pl.BlockSpec((None, seq // CHUNK, LANES, CHUNK), lambda i: (i, 0, 0, 0)),
        ],
        out_shape=[
            jax.ShapeDtypeStruct((t, N_PAIRS * LANES), BF16),
            jax.ShapeDtypeStruct((b, seq // CHUNK, LANES, CHUNK), F32),
        ],
        compiler_params=_cparams(1),
        name="fox_scan",
    )(small, fb_row, jnp.asarray(selk, BF16), jnp.asarray(konst, F32), jnp.asarray(selq, BF16),
      jnp.asarray(qonst, F32))


FOX_BLK = 256
FOX_PAIRS = 2
FOX_HEADS = 2 * FOX_PAIRS
FOX_UNROLL = 4
FOX_VROWS = HEAD_DIM + 16


def _fox_schedule(n_blocks):
    items = [(q, q) for q in range(n_blocks)] + [(q, j) for q in range(n_blocks) for j in range(q)]
    return np.asarray([it[0] for it in items], np.int32), np.asarray([it[1] for it in items], np.int32)


def _fox_kernel(tq_ref, tj_ref, q_ref, k_ref, v_ref, kf_ref, qf_ref, z_ref, o_ref,
                    kaug_ref, vt_ref, qt_ref, acc_ref, m_ref, a_ref, mx_ref, st_ref, p_ref):
    seq = k_ref.shape[0]
    nb = FOX_BLK
    n_blocks = seq // nb
    n_items = n_blocks * (n_blocks + 1) // 2
    blk_rows = lambda i: pl.ds(pl.multiple_of(i * nb, nb), nb)

    lane = lax.broadcasted_iota(jnp.int32, (nb, LANES), 1)
    ones = jnp.ones((FOX_VROWS - HEAD_DIM, nb), BF16)
    zeros56 = jnp.zeros((HEAD_DIM - 8, nb), F32)

    def setup(c, carry):
        rows = blk_rows(c)
        for pr in range(FOX_PAIRS):
            cols = slice(pr * LANES, (pr + 1) * LANES)
            k2 = k_ref[rows, cols]
            kf = kf_ref[rows, cols]
            kaug_ref[2 * pr, rows, :] = jnp.where(lane < HEAD_DIM, k2, kf)
            kaug_ref[2 * pr + 1, rows, :] = jnp.where(lane >= HEAD_DIM, k2, kf)
            vt = v_ref[rows, cols].astype(F32).T.astype(BF16)
            vt_ref[2 * pr, c] = jnp.concatenate([vt[:HEAD_DIM], ones], axis=0)
            vt_ref[2 * pr + 1, c] = jnp.concatenate([vt[HEAD_DIM:], ones], axis=0)
            qt = q_ref[rows, cols].astype(F32).T
            qf = jnp.concatenate([qf_ref[(nb // CHUNK) * c + s, 16 * pr:16 * (pr + 1), :]
                                  for s in range(nb // CHUNK)], axis=1)
            qt_ref[c, 2 * pr] = jnp.concatenate([qt[:HEAD_DIM], qf[:8], zeros56], axis=0).astype(BF16)
            qt_ref[c, 2 * pr + 1] = jnp.concatenate([qf[8:], zeros56, qt[HEAD_DIM:]], axis=0).astype(BF16)
        return carry

    lax.fori_loop(0, n_blocks, setup, 0)
    acc_ref[...] = jnp.zeros_like(acc_ref)
    m_ref[...] = jnp.full(m_ref.shape, NEG, F32)

    def scores(u, slot):
        qb, kb = tq_ref[u], tj_ref[u]
        for h in range(FOX_HEADS):
            st = _dot(kaug_ref[h, blk_rows(kb), :], qt_ref[qb, h])
            st_ref[slot, h] = st
            mx_ref[slot, h] = jnp.max(st, axis=0, keepdims=True)

    def softmax(u, slot, masked):
        qb = tq_ref[u]
        if masked:
            keep = (lax.broadcasted_iota(jnp.int32, (nb, nb), 0) <= lax.broadcasted_iota(jnp.int32, (nb, nb), 1))
        for h in range(FOX_HEADS):
            st = st_ref[slot, h]
            if masked:
                st = jnp.where(keep, st, NEG)
                blk_max = jnp.max(st, axis=0, keepdims=True)
            else:
                blk_max = mx_ref[slot, h]
            m_old = m_ref[qb, h]
            m_new = jnp.maximum(m_old, blk_max)
            p_ref[slot, h] = jnp.exp2(st - m_new).astype(BF16)
            a_ref[slot, h] = jnp.exp2(m_old - m_new)
            m_ref[qb, h] = m_new

    def values(u, slot):
        qb, kb = tq_ref[u], tj_ref[u]
        for h in range(FOX_HEADS):
            acc_ref[qb, h] = a_ref[slot, h] * acc_ref[qb, h] + _dot(vt_ref[h, kb], p_ref[slot, h])

    def step(u, slot, masked):
        values(u - 2, slot)
        softmax(u - 1, 1 - slot, masked)
        scores(u, slot)

    def run(lo, hi, masked):
        while lo < hi and lo % FOX_UNROLL:
            step(lo, lo % 2, masked)
            lo += 1
        trips = (hi - lo) // FOX_UNROLL
        if trips:
            def trip(g, carry):
                for i in range(FOX_UNROLL):
                    step(lo + g * FOX_UNROLL + i, i % 2, masked)
                return carry
            lax.fori_loop(0, trips, trip, 0)
        for u in range(lo + trips * FOX_UNROLL, hi):
            step(u, u % 2, masked)

    scores(0, 0)
    scores(1, 1)
    softmax(0, 0, True)
    run(2, n_blocks + 1, True)
    run(n_blocks + 1, n_items, False)
    values(n_items - 2, n_items % 2)
    softmax(n_items - 1, (n_items - 1) % 2, False)
    values(n_items - 1, (n_items - 1) % 2)

    def finish(c, carry):
        rows = blk_rows(c)
        for pr in range(FOX_PAIRS):
            ot = jnp.concatenate([acc_ref[c, 2 * pr + e, :HEAD_DIM, :]
                                  * (1.0 / acc_ref[c, 2 * pr + e, HEAD_DIM:HEAD_DIM + 1, :]) for e in range(2)], axis=0)
            cols = slice(pr * LANES, (pr + 1) * LANES)
            o_ref[rows, cols] = (ot.T * _silu(z_ref[rows, cols].astype(F32))).astype(BF16)
        return carry

    lax.fori_loop(0, n_blocks, finish, 0)


def _fox_attention(proj, kfeat, qfeat_t, seq):
    t = proj.shape[0]
    b = t // seq
    nblk = seq // FOX_BLK
    w = FOX_PAIRS * LANES
    cb = lambda off: off // w
    tq, tj = _fox_schedule(nblk)
    col = lambda off: pl.BlockSpec((seq, w), lambda bi, p: (bi, cb(off) + p))
    return pl.pallas_call(
        _fox_kernel,
        grid=(b, N_PAIRS // FOX_PAIRS),
        in_specs=[
            pl.BlockSpec(memory_space=pltpu.SMEM), pl.BlockSpec(memory_space=pltpu.SMEM),
            col(COL_C_Q), col(COL_C_K), col(COL_C_V),
            pl.BlockSpec((seq, w), lambda bi, p: (bi, p)),
            pl.BlockSpec((None, seq // CHUNK, 16 * FOX_PAIRS, CHUNK), lambda bi, p: (bi, 0, p, 0)),
            col(COL_C_Z),
        ],
        out_specs=pl.BlockSpec((seq, w), lambda bi, p: (bi, p)),
        out_shape=jax.ShapeDtypeStruct((t, D_MODEL), BF16),
        scratch_shapes=[
            pltpu.VMEM((FOX_HEADS, seq, LANES), BF16),
            pltpu.VMEM((FOX_HEADS, nblk, FOX_VROWS, FOX_BLK), BF16),
            pltpu.VMEM((nblk, FOX_HEADS, LANES, FOX_BLK), BF16),
            pltpu.VMEM((nblk, FOX_HEADS, FOX_VROWS, FOX_BLK), F32),
            pltpu.VMEM((nblk, FOX_HEADS, 1, FOX_BLK), F32),
            pltpu.VMEM((2, FOX_HEADS, 1, FOX_BLK), F32),
            pltpu.VMEM((2, FOX_HEADS, 1, FOX_BLK), F32),
            pltpu.VMEM((2, FOX_HEADS, FOX_BLK, FOX_BLK), F32),
            pltpu.VMEM((2, FOX_HEADS, FOX_BLK, FOX_BLK), BF16),
        ],
        compiler_params=_cparams(2),
        name="fox_attn",
    )(jnp.asarray(tq), jnp.asarray(tj), proj, proj, proj, kfeat, qfeat_t, proj)


SWA_SUB = 1


def _rope(x, cos, sin_signed, first_half):
    partner = jnp.where(first_half, pltpu.roll(x, 96, 1), pltpu.roll(x, 32, 1))
    return x * cos + partner * sin_signed


def _swa_kernel(sinks_ref, q_ref, kc_ref, kp_ref, vc_ref, vp_ref, z_ref, cosc_ref, sinc_ref, cosp_ref, sinp_ref,
                o_ref):
    i = pl.program_id(1)
    blk = CHUNK
    lane = lax.broadcasted_iota(jnp.int32, (blk, LANES), 1)
    first_half = (lane % HEAD_DIM) < (HEAD_DIM // 2)
    even_head = lane < HEAD_DIM

    qidx = lax.broadcasted_iota(jnp.int32, (blk, 2 * blk), 0)
    sidx = lax.broadcasted_iota(jnp.int32, (blk, 2 * blk), 1)
    band = (sidx > qidx) & (sidx <= qidx + blk)
    first_key = jnp.where(i > 0, 0, blk)
    band_first = band & (sidx >= first_key)

    for kv in range(SWA_KV_HEADS):
        cols = slice(kv * LANES, (kv + 1) * LANES)
        ks = [_rope(kp_ref[:, cols].astype(F32), cosp_ref[...], sinp_ref[...], first_half).astype(BF16)]
        vs = [vp_ref[:, cols]]
        for s in range(SWA_SUB):
            rows = slice(s * blk, (s + 1) * blk)
            ks.append(_rope(kc_ref[rows, cols].astype(F32), cosc_ref[rows, :], sinc_ref[rows, :],
                            first_half).astype(BF16))
            vs.append(vc_ref[rows, cols])
        for s in range(SWA_SUB):
            rows = slice(s * blk, (s + 1) * blk)
            k_all = jnp.concatenate([ks[s], ks[s + 1]], axis=0)
            v_all = jnp.concatenate([vs[s], vs[s + 1]], axis=0)
            keep = band_first if s == 0 else band
            for pp in range(2):
                pair = 2 * kv + pp
                pcols = slice(pair * LANES, (pair + 1) * LANES)
                qr = _rope(q_ref[rows, pcols].astype(F32), cosc_ref[rows, :], sinc_ref[rows, :], first_half)
                outs = []
                for e in range(2):
                    sel = even_head if e == 0 else jnp.logical_not(even_head)
                    qh = jnp.where(sel, qr, 0.0).astype(BF16)
                    sc = jnp.where(keep, _dot_nt(qh, k_all), NEG)
                    sink = sinks_ref[2 * pair + e] * LOG2E
                    m = jnp.maximum(jnp.max(sc, axis=-1, keepdims=True), sink)
                    p = jnp.exp2(sc - m)
                    denom = jnp.sum(p, axis=-1, keepdims=True) + jnp.exp2(sink - m)
                    outs.append(_dot(p.astype(BF16), v_all) * (1.0 / denom))
                y = jnp.where(even_head, outs[0], outs[1])
                o_ref[rows, pcols] = (y * _silu(z_ref[rows, pcols].astype(F32))).astype(BF16)


def _swa_attention(proj, sinks, cos_t, sin_t, seq):
    t = proj.shape[0]
    b = t // seq
    nblk = seq // CHUNK
    tm = SWA_SUB * CHUNK
    nstep = seq // tm
    kvw = SWA_KV_HEADS * LANES
    prev_blk = lambda i: jnp.maximum(SWA_SUB * i - 1, 0)
    cur = lambda w, off: pl.BlockSpec((tm, w), lambda bi, i: (bi * nstep + i, off // w))
    prev = lambda w, off: pl.BlockSpec((CHUNK, w), lambda bi, i: (bi * nblk + prev_blk(i), off // w))
    return pl.pallas_call(
        _swa_kernel,
        grid=(b, nstep),
        in_specs=[
            pl.BlockSpec(memory_space=pltpu.SMEM),
            cur(D_MODEL, COL_B_Q),
            cur(kvw, COL_B_K), prev(kvw, COL_B_K),
            cur(kvw, COL_B_V), prev(kvw, COL_B_V),
            cur(D_MODEL, COL_B_Z),
            pl.BlockSpec((tm, LANES), lambda bi, i: (i, 0)),
            pl.BlockSpec((tm, LANES), lambda bi, i: (i, 0)),
            pl.BlockSpec((CHUNK, LANES), lambda bi, i: (prev_blk(i), 0)),
            pl.BlockSpec((CHUNK, LANES), lambda bi, i: (prev_blk(i), 0)),
        ],
        out_specs=pl.BlockSpec((tm, D_MODEL), lambda bi, i: (bi * nstep + i, 0)),
        out_shape=jax.ShapeDtypeStruct((t, D_MODEL), BF16),
        compiler_params=_cparams(2),
        name="swa_attn",
    )(sinks, proj, proj, proj, proj, proj, proj, cos_t, sin_t, cos_t, sin_t)


MAMBA_SUB = 2


def _mamba_kernel(xbc_ref, z_ref, sm_ref, cw_ref, cb_ref, dtb_ref, alog_ref, dsk_ref, nw_ref, o_ref,
                  prev_ref, state_ref):
    @pl.when(pl.program_id(1) == 0)
    def _():
        prev_ref[...] = jnp.zeros_like(prev_ref)
        state_ref[...] = jnp.zeros_like(state_ref)

    prev = prev_ref[...]
    for sub in range(MAMBA_SUB):
        rows = slice(sub * CHUNK, (sub + 1) * CHUNK)
        _mamba_chunk(xbc_ref, z_ref, sm_ref, cw_ref, cb_ref, dtb_ref, alog_ref, dsk_ref, nw_ref, o_ref, state_ref,
                     rows, prev)
        prev = xbc_ref[rows, :]
    prev_ref[...] = prev


def _mamba_chunk(xbc_ref, z_ref, sm_ref, cw_ref, cb_ref, dtb_ref, alog_ref, dsk_ref, nw_ref, o_ref, state_ref,
                 rows, prev):
    n = CHUNK
    cur = xbc_ref[rows, :]
    both = jnp.concatenate([prev, cur], axis=0)
    srow = lax.broadcasted_iota(jnp.int32, (n, 2 * n), 0)
    scol = lax.broadcasted_iota(jnp.int32, (n, 2 * n), 1)
    shifts = jnp.concatenate([(scol == srow + (n - (CONV_WIDTH - 1) + k)).astype(BF16)
                              for k in range(CONV_WIDTH - 1)], axis=0)
    shifted = _dot(shifts, both)
    conv = cb_ref[...] + cw_ref[CONV_WIDTH - 1:CONV_WIDTH, :] * cur.astype(F32)
    for k in range(CONV_WIDTH - 1):
        conv = conv + cw_ref[k:k + 1, :] * shifted[k * n:(k + 1) * n, :]
    xbc = _silu(conv)
    xs = xbc[:, :D_MODEL]
    bm = xbc[:, D_MODEL:D_MODEL + SSM_GROUPS * SSM_STATE]
    cm = xbc[:, D_MODEL + SSM_GROUPS * SSM_STATE:]
    xs_b = xs.astype(BF16)

    dt = _softplus(sm_ref[rows, :] + dtb_ref[...])
    a_dt = dt * (jnp.exp(alog_ref[...]) * (-LOG2E))
    row = lax.broadcasted_iota(jnp.int32, (n, n), 0)
    col = lax.broadcasted_iota(jnp.int32, (n, n), 1)
    causal = row >= col
    tri = causal.astype(BF16)
    loc = _dot(tri, jnp.concatenate(_split3(a_dt), axis=1))
    acum = loc[:, :LANES] + loc[:, LANES:2 * LANES] + loc[:, 2 * LANES:]
    acum_t = acum.T
    dt_t = dt.T
    eacum = jnp.exp2(acum)
    lane = lax.broadcasted_iota(jnp.int32, (n, LANES), 1)
    even_head = lane < HEAD_DIM

    ys = []
    for g in range(SSM_GROUPS):
        gcols = slice(g * SSM_STATE, (g + 1) * SSM_STATE)
        bg = bm[:, gcols]
        cg = cm[:, gcols]
        cbm = _dot_nt(cg.astype(BF16), bg.astype(BF16))
        bg_t = bg.T
        for pp in range(2):
            pair = 2 * g + pp
            pcols = slice(pair * LANES, (pair + 1) * LANES)
            st_old = state_ref[:, pcols]
            rhs = jnp.concatenate([xs_b[:, pcols], st_old.astype(BF16)], axis=0)
            outs, news, sdec = [], [], []
            for e in range(2):
                h = 2 * pair + e
                a_col = acum[:, h:h + 1]
                a_row = acum_t[h:h + 1, :]
                dt_row = dt_t[h:h + 1, :]
                decay = jnp.exp2(jnp.where(causal, a_col - a_row, NEG))
                m_in = (cbm * decay * dt_row).astype(BF16)
                c_off = (cg * eacum[:, h:h + 1]).astype(BF16)
                outs.append(_dot(jnp.concatenate([m_in, c_off], axis=1), rhs))
                a_last = a_row[:, n - 1:n]
                w_row = jnp.exp2(a_last - a_row) * dt_row
                news.append(_dot((bg_t * w_row).astype(BF16), xs_b[:, pcols]))
                sdec.append(jnp.exp2(a_last))
            ys.append(jnp.where(even_head, outs[0], outs[1]))
            state_ref[:, pcols] = (st_old * jnp.where(even_head, sdec[0], sdec[1])
                                   + jnp.where(even_head, news[0], news[1]))

    y = jnp.concatenate(ys, axis=1) + dsk_ref[...] * xs
    y = y * _silu(z_ref[rows, :].astype(F32))
    gw = D_MODEL // SSM_GROUPS
    for g in range(SSM_GROUPS):
        gc = slice(g * gw, (g + 1) * gw)
        yg = y[:, gc]
        ms = jnp.mean(yg * yg, axis=-1, keepdims=True)
        o_ref[rows, gc] = (yg * lax.rsqrt(ms + NORM_EPS) * nw_ref[:, gc]).astype(BF16)


def _mamba(proj, small, conv_w, conv_b, dtb_row, alog_row, dskip_full, norm_w, seq):
    t = proj.shape[0]
    b = t // seq
    tm = MAMBA_SUB * CHUNK
    nchunk = seq // tm
    xw = 2 * D_MODEL
    const = lambda shape: pl.BlockSpec(shape, lambda bi, c: (0,) * len(shape))
    return pl.pallas_call(
        _mamba_kernel,
        grid=(b, nchunk),
        in_specs=[
            pl.BlockSpec((tm, xw), lambda bi, c: (bi * nchunk + c, COL_A_XBC // xw)),
            pl.BlockSpec((tm, D_MODEL), lambda bi, c: (bi * nchunk + c, COL_A_Z // D_MODEL)),
            pl.BlockSpec((tm, LANES), lambda bi, c: (bi * nchunk + c, 0)),
            const((CONV_WIDTH, xw)), const((1, xw)), const((1, LANES)), const((1, LANES)),
            const((1, D_MODEL)), const((1, D_MODEL)),
        ],
        out_specs=pl.BlockSpec((tm, D_MODEL), lambda bi, c: (bi * nchunk + c, 0)),
        out_shape=jax.ShapeDtypeStruct((t, D_MODEL), BF16),
        scratch_shapes=[
            pltpu.VMEM((CHUNK, xw), BF16),
            pltpu.VMEM((SSM_STATE, D_MODEL), F32),
        ],
        compiler_params=_cparams(2),
        name="mamba",
    )(proj, proj, small, conv_w, conv_b, dtb_row, alog_row, dskip_full, norm_w)


OUT_TM = 512


def _merge_kernel(x_ref, ya_ref, yb_ref, yc_ref, g_ref, gb_ref, wp_ref, wo_ref, fnw_ref, o_ref, *, final_norm):
    merged = None
    for i, y_ref in enumerate((ya_ref, yb_ref, yc_ref)):
        branch = _dot(y_ref[...], wp_ref[i])
        gcols = slice(i * D_MODEL, (i + 1) * D_MODEL)
        gate = 1.0 / (1.0 + jnp.exp2((g_ref[:, gcols].astype(F32) + gb_ref[:, gcols]) * (-LOG2E)))
        merged = gate * branch if merged is None else merged + gate * branch
    x = x_ref[...] + _dot(merged.astype(BF16), wo_ref[...])
    if final_norm:
        ms = jnp.mean(x * x, axis=-1, keepdims=True)
        x = x * lax.rsqrt(ms + NORM_EPS) * fnw_ref[...]
    o_ref[...] = x


def _merge(x2, ya, yb, yc, proj, gate_bias, w_proj, w_out, final_norm_w, final_norm):
    t = x2.shape[0]
    gw = 3 * D_MODEL
    row = lambda w, cbi: pl.BlockSpec((OUT_TM, w), lambda i: (i, cbi))
    const = lambda shape: pl.BlockSpec(shape, lambda i: (0,) * len(shape))
    return pl.pallas_call(
        functools.partial(_merge_kernel, final_norm=final_norm),
        grid=(t // OUT_TM,),
        in_specs=[
            row(D_MODEL, 0), row(D_MODEL, 0), row(D_MODEL, 0), row(D_MODEL, 0),
            row(gw, COL_GATES // gw),
            const((1, gw)), const((3, D_MODEL, D_MODEL)), const((D_MODEL, D_MODEL)), const((1, D_MODEL)),
        ],
        out_specs=row(D_MODEL, 0),
        out_shape=jax.ShapeDtypeStruct((t, D_MODEL), F32),
        compiler_params=_cparams(1),
        name="merge_out",
    )(x2, ya, yb, yc, proj, gate_bias, w_proj, w_out, final_norm_w)


def _dup_heads(w):
    d = w.shape[0]
    w4 = w.reshape(d, SWA_KV_HEADS, HEAD_DIM)
    return jnp.concatenate([w4, w4], axis=-1).reshape(d, SWA_KV_HEADS * LANES)


def _layout_w_in(w):
    offs = np.cumsum((0,) + _IN_SIZES)
    (a_xbc, a_z, a_dt, b_q, b_k, b_v, b_z, c_q, c_k, c_v, c_f, c_z, gates) = [
        w[:, offs[i]:offs[i + 1]] for i in range(len(_IN_SIZES))]
    scale = HEAD_DIM ** -0.5
    big = jnp.concatenate([a_xbc, a_z, b_z, c_z, b_q * (scale * LOG2E), c_q * (scale * LOG2E), c_k, c_v, gates,
                           _dup_heads(b_k), _dup_heads(b_v)], axis=1).astype(BF16)
    small = jnp.concatenate([a_dt, c_f, jnp.zeros((w.shape[0], LANES - 32), w.dtype)], axis=1)
    s_hi = small.astype(BF16)
    s_lo = (small - s_hi.astype(F32)).astype(BF16)
    return big, jnp.stack([s_hi, s_lo])


def _lane_row(v, start):
    return jnp.zeros((1, LANES), F32).at[0, start:start + N_HEADS].set(v.astype(F32))


def _rope_tables(seq):
    pos = jnp.arange(seq, dtype=F32)
    inv_freq = ROPE_THETA ** (-jnp.arange(0, HEAD_DIM, 2, dtype=F32) / HEAD_DIM)
    ang = pos[:, None] * inv_freq[None, :]
    cos, sin = jnp.cos(ang), jnp.sin(ang)
    cos_t = jnp.concatenate([cos, cos, cos, cos], axis=1)
    sin_t = jnp.concatenate([-sin, sin, -sin, sin], axis=1)
    return cos_t, sin_t


def kernel(x, norm_w, w_in, conv_w, conv_b, dt_bias, a_log, d_skip, ssm_norm_w,
           sinks, f_bias, gate_bias, w_proj, w_out, final_norm_w):
    b, s, d = x.shape
    depth = norm_w.shape[0]
    assert d == D_MODEL and s % FOX_BLK == 0 and s >= 2 * FOX_BLK and (b * s) % IN_TM == 0
    cos_t, sin_t = _rope_tables(s)
    x2 = x.reshape(b * s, d)
    for layer in range(depth):
        w_big, w_small = _layout_w_in(w_in[layer])
        proj, small = _inproj(x2, norm_w[layer][None, :], w_big, w_small)
        kfeat, qfeat_t = _fox_scan(small, _lane_row(f_bias[layer], LANE_F), s)
        ya = _mamba(proj, small, conv_w[layer], conv_b[layer][None, :], _lane_row(dt_bias[layer], 0),
                    _lane_row(a_log[layer], 0), jnp.repeat(d_skip[layer], HEAD_DIM)[None, :],
                    ssm_norm_w[layer][None, :], s)
        yb = _swa_attention(proj, sinks[layer], cos_t, sin_t, s)
        yc = _fox_attention(proj, kfeat, qfeat_t, s)
        x2 = _merge(x2, ya, yb, yc, proj, gate_bias[layer].reshape(1, 3 * d),
                    w_proj[layer].astype(BF16), w_out[layer].astype(BF16), final_norm_w[None, :],
                    final_norm=(layer == depth - 1))
    return x2.reshape(b, s, d)
```

```python
import functools
import math

import jax
import jax.numpy as jnp
import numpy as np
from jax import lax
from jax.experimental import pallas as pl
from jax.experimental.pallas import tpu as pltpu

F32 = jnp.float32
BF16 = jnp.bfloat16

D_MODEL = 1024
HEAD_DIM = 64
N_HEADS = 16
N_PAIRS = N_HEADS // 2
LANES = 128
SSM_GROUPS = 4
SSM_STATE = 128
CONV_WIDTH = 4
CHUNK = 128
SWA_KV_HEADS = 4
ROPE_THETA = 10000.0
NORM_EPS = 1e-6
NEG = -1e30
LOG2E = math.log2(math.e)

COL_A_XBC = 0
COL_A_Z = 2048
COL_B_Z = 3072
COL_C_Z = 4096
COL_B_Q = 5120
COL_C_Q = 6144
COL_C_K = 7168
COL_C_V = 8192
COL_GATES = 9216
COL_B_K = 12288
COL_B_V = 12800
N_PROJ = 13312

_IN_SIZES = (2048, 1024, 16, 1024, 256, 256, 1024, 1024, 1024, 1024, 16, 1024, 3072)

VMEM_LIMIT = 56 * 1024 * 1024


def _cparams(n_axes):
    return pltpu.CompilerParams(dimension_semantics=("arbitrary",) * n_axes, vmem_limit_bytes=VMEM_LIMIT)


def _split3(x):
    hi = x.astype(BF16)
    r1 = x - hi.astype(F32)
    mid = r1.astype(BF16)
    lo = (r1 - mid.astype(F32)).astype(BF16)
    return hi, mid, lo


def _dot(a, b):
    return jnp.dot(a, b, preferred_element_type=F32)


def _dot_nt(a, b):
    return lax.dot_general(a, b, (((1,), (1,)), ((), ())), preferred_element_type=F32)


def _silu(x):
    return x * (1.0 / (1.0 + jnp.exp2(x * (-LOG2E))))


def _softplus(x):
    return jnp.maximum(x, 0.0) + jnp.log1p(jnp.exp(-jnp.abs(x)))


IN_TM = 2048
IN_TN = 1024
IN_NORM_ROWS = 512


def _inproj_kernel(x_ref, nw_ref, w_ref, wsm_ref, o_ref, osm_ref, h_ref):
    @pl.when(pl.program_id(1) == 0)
    def _():
        for r in range(IN_TM // IN_NORM_ROWS):
            rows = slice(r * IN_NORM_ROWS, (r + 1) * IN_NORM_ROWS)
            x = x_ref[rows, :]
            ms = jnp.mean(x * x, axis=-1, keepdims=True)
            h = x * lax.rsqrt(ms + NORM_EPS) * nw_ref[...]
            hh = h.astype(BF16)
            h_ref[rows, :] = hh
            hl = (h - hh.astype(F32)).astype(BF16)
            osm_ref[rows, :] = _dot(hh, wsm_ref[0]) + _dot(hl, wsm_ref[0]) + _dot(hh, wsm_ref[1])

    o_ref[...] = _dot(h_ref[...], w_ref[...]).astype(BF16)


def _inproj(x2, norm_w, w_big, w_small):
    t = x2.shape[0]
    grid = (t // IN_TM, N_PROJ // IN_TN)
    return pl.pallas_call(
        _inproj_kernel,
        grid=grid,
        in_specs=[
            pl.BlockSpec((IN_TM, D_MODEL), lambda i, j: (i, 0)),
            pl.BlockSpec((1, D_MODEL), lambda i, j: (0, 0)),
            pl.BlockSpec((D_MODEL, IN_TN), lambda i, j: (0, j)),
            pl.BlockSpec((2, D_MODEL, LANES), lambda i, j: (0, 0, 0)),
        ],
        out_specs=[
            pl.BlockSpec((IN_TM, IN_TN), lambda i, j: (i, j)),
            pl.BlockSpec((IN_TM, LANES), lambda i, j: (i, 0)),
        ],
        out_shape=[
            jax.ShapeDtypeStruct((t, N_PROJ), BF16),
            jax.ShapeDtypeStruct((t, LANES), F32),
        ],
        scratch_shapes=[pltpu.VMEM((IN_TM, D_MODEL), BF16)],
        compiler_params=_cparams(2),
        name="inproj",
    )(x2, norm_w, w_big, w_small)


LANE_F = 16


def _scan_selectors():
    selk = np.zeros((3 * LANES, N_PAIRS * LANES), np.float32)
    konst = np.zeros((1, N_PAIRS * LANES), np.float32)
    selq = np.zeros((LANES, 3 * LANES), np.float32)
    qonst = np.zeros((LANES, 1), np.float32)
    for p in range(N_PAIRS):
        for e in range(2):
            src = LANE_F + 2 * p + e
            kbase = p * LANES + (64 if e == 0 else 0)
            qbase = 16 * p + 8 * e
            for t in range(3):
                konst[0, kbase + t] = 1.0
                selk[t * LANES + src, kbase + 3 + t] = 1.0
                selq[qbase + t, t * LANES + src] = 1.0
                qonst[qbase + 3 + t, 0] = -1.0
    return selk, konst, selq, qonst


SCAN_UNROLL = 4


def _exact_prefix(mask_bf16, x):
    y = _dot(mask_bf16, jnp.concatenate(_split3(x), axis=1))
    return y[:, :LANES] + y[:, LANES:2 * LANES] + y[:, 2 * LANES:]


def _scan_kernel(sm_ref, fb_ref, selk_ref, konst_ref, selq_ref, qonst_ref, kf_ref, qf_ref, loc_ref, tot_ref):
    nchunk = sm_ref.shape[0] // CHUNK
    assert nchunk % SCAN_UNROLL == 0
    row = lax.broadcasted_iota(jnp.int32, (CHUNK, CHUNK), 0)
    col = lax.broadcasted_iota(jnp.int32, (CHUNK, CHUNK), 1)
    tri = (row >= col).astype(BF16)
    fb = fb_ref[...]
    chunk_rows = lambda c: pl.ds(pl.multiple_of(c * CHUNK, CHUNK), CHUNK)

    def unrolled(body):
        def trip(g, carry):
            for i in range(SCAN_UNROLL):
                body(g * SCAN_UNROLL + i)
            return carry
        lax.fori_loop(0, nchunk // SCAN_UNROLL, trip, 0)

    def local_scan(c):
        v = sm_ref[chunk_rows(c), :] + fb
        lf = (jnp.minimum(v, 0.0) - jnp.log1p(jnp.exp(-jnp.abs(v)))) * LOG2E
        loc = _exact_prefix(tri, lf)
        loc_ref[chunk_rows(c), :] = loc
        tot_ref[pl.ds(c, 1), :] = loc[CHUNK - 1:CHUNK, :]

    unrolled(local_scan)

    crow = lax.broadcasted_iota(jnp.int32, (nchunk, nchunk), 0)
    ccol = lax.broadcasted_iota(jnp.int32, (nchunk, nchunk), 1)
    tot_ref[...] = _exact_prefix((crow > ccol).astype(BF16), tot_ref[...])

    def emit(c):
        cum = loc_ref[chunk_rows(c), :] + tot_ref[pl.ds(c, 1), :]
        c3 = jnp.concatenate(_split3(cum), axis=1)
        kf_ref[chunk_rows(c), :] = (_dot(c3, selk_ref[...]) + konst_ref[...]).astype(BF16)
        qf_ref[c] = _dot_nt(selq_ref[...], c3) + qonst_ref[...]

    unrolled(emit)


def _fox_scan(small, fb_row, seq):
    t = small.shape[0]
    b = t // seq
    selk, konst, selq, qonst = _scan_selectors()
    const = lambda shape: pl.BlockSpec(shape, lambda i: (0,) * len(shape))
    return pl.pallas_call(
        _scan_kernel,
        grid=(b,),
        in_specs=[
            pl.BlockSpec((seq, LANES), lambda i: (i, 0)),
            const((1, LANES)),
            const(selk.shape), const(konst.shape), const(selq.shape), const(qonst.shape),
        ],
        out_specs=[
            pl.BlockSpec((seq, N_PAIRS * LANES), lambda i: (i, 0)),
            pl.BlockSpec((None, seq // CHUNK, LANES, CHUNK), lambda i: (i, 0, 0, 0)),
        ],
        out_shape=[
            jax.ShapeDtypeStruct((t, N_PAIRS * LANES), BF16),
            jax.ShapeDtypeStruct((b, seq // CHUNK, LANES, CHUNK), F32),
        ],
        scratch_shapes=[
            pltpu.VMEM((seq, LANES), F32),
            pltpu.VMEM((seq // CHUNK, LANES), F32),
        ],
        compiler_params=_cparams(1),
        name="fox_scan",
    )(small, fb_row, jnp.asarray(selk, BF16), jnp.asarray(konst, F32), jnp.asarray(selq, BF16),
      jnp.asarray(qonst, F32))


FOX_BLK = 256
FOX_PAIRS = 2
FOX_HEADS = 2 * FOX_PAIRS
FOX_UNROLL = 4
FOX_VROWS = HEAD_DIM + 16


def _fox_schedule(n_blocks):
    items = [(q, q) for q in range(n_blocks)] + [(q, j) for q in range(n_blocks) for j in range(q)]
    return np.asarray([it[0] for it in items], np.int32), np.asarray([it[1] for it in items], np.int32)


def _fox_kernel(tq_ref, tj_ref, q_ref, k_ref, v_ref, kf_ref, qf_ref, z_ref, o_ref,
                    kaug_ref, vt_ref, qt_ref, acc_ref, m_ref, a_ref, mx_ref, st_ref, p_ref):
    seq = k_ref.shape[0]
    nb = FOX_BLK
    n_blocks = seq // nb
    n_items = n_blocks * (n_blocks + 1) // 2
    blk_rows = lambda i: pl.ds(pl.multiple_of(i * nb, nb), nb)

    lane = lax.broadcasted_iota(jnp.int32, (nb, LANES), 1)
    ones = jnp.ones((FOX_VROWS - HEAD_DIM, nb), BF16)
    zeros56 = jnp.zeros((HEAD_DIM - 8, nb), F32)

    def setup(c, carry):
        rows = blk_rows(c)
        for pr in range(FOX_PAIRS):
            cols = slice(pr * LANES, (pr + 1) * LANES)
            k2 = k_ref[rows, cols]
            kf = kf_ref[rows, cols]
            kaug_ref[2 * pr, rows, :] = jnp.where(lane < HEAD_DIM, k2, kf)
            kaug_ref[2 * pr + 1, rows, :] = jnp.where(lane >= HEAD_DIM, k2, kf)
            vt = v_ref[rows, cols].astype(F32).T.astype(BF16)
            vt_ref[2 * pr, c] = jnp.concatenate([vt[:HEAD_DIM], ones], axis=0)
            vt_ref[2 * pr + 1, c] = jnp.concatenate([vt[HEAD_DIM:], ones], axis=0)
            qt = q_ref[rows, cols].astype(F32).T
            qf = jnp.concatenate([qf_ref[(nb // CHUNK) * c + s, 16 * pr:16 * (pr + 1), :]
                                  for s in range(nb // CHUNK)], axis=1)
            qt_ref[c, 2 * pr] = jnp.concatenate([qt[:HEAD_DIM], qf[:8], zeros56], axis=0).astype(BF16)
            qt_ref[c, 2 * pr + 1] = jnp.concatenate([qf[8:], zeros56, qt[HEAD_DIM:]], axis=0).astype(BF16)
        return carry

    lax.fori_loop(0, n_blocks, setup, 0)
    acc_ref[...] = jnp.zeros_like(acc_ref)
    m_ref[...] = jnp.full(m_ref.shape, NEG, F32)

    def scores(u, slot):
        qb, kb = tq_ref[u], tj_ref[u]
        for h in range(FOX_HEADS):
            st = _dot(kaug_ref[h, blk_rows(kb), :], qt_ref[qb, h])
            st_ref[slot, h] = st
            mx_ref[slot, h] = jnp.max(st, axis=0, keepdims=True)

    def softmax(u, slot, masked):
        qb = tq_ref[u]
        if masked:
            keep = (lax.broadcasted_iota(jnp.int32, (nb, nb), 0) <= lax.broadcasted_iota(jnp.int32, (nb, nb), 1))
        for h in range(FOX_HEADS):
            st = st_ref[slot, h]
            if masked:
                st = jnp.where(keep, st, NEG)
                blk_max = jnp.max(st, axis=0, keepdims=True)
            else:
                blk_max = mx_ref[slot, h]
            m_old = m_ref[qb, h]
            m_new = jnp.maximum(m_old, blk_max)
            p_ref[slot, h] = jnp.exp2(st - m_new).astype(BF16)
            a_ref[slot, h] = jnp.exp2(m_old - m_new)
            m_ref[qb, h] = m_new

    def values(u, slot):
        qb, kb = tq_ref[u], tj_ref[u]
        for h in range(FOX_HEADS):
            acc_ref[qb, h] = a_ref[slot, h] * acc_ref[qb, h] + _dot(vt_ref[h, kb], p_ref[slot, h])

    def step(u, slot, masked):
        values(u - 2, slot)
        softmax(u - 1, 1 - slot, masked)
        scores(u, slot)

    def run(lo, hi, masked):
        while lo < hi and lo % FOX_UNROLL:
            step(lo, lo % 2, masked)
            lo += 1
        trips = (hi - lo) // FOX_UNROLL
        if trips:
            def trip(g, carry):
                for i in range(FOX_UNROLL):
                    step(lo + g * FOX_UNROLL + i, i % 2, masked)
                return carry
            lax.fori_loop(0, trips, trip, 0)
        for u in range(lo + trips * FOX_UNROLL, hi):
            step(u, u % 2, masked)

    scores(0, 0)
    scores(1, 1)
    softmax(0, 0, True)
    run(2, n_blocks + 1, True)
    run(n_blocks + 1, n_items, False)
    values(n_items - 2, n_items % 2)
    softmax(n_items - 1, (n_items - 1) % 2, False)
    values(n_items - 1, (n_items - 1) % 2)

    def finish(c, carry):
        rows = blk_rows(c)
        for pr in range(FOX_PAIRS):
            ot = jnp.concatenate([acc_ref[c, 2 * pr + e, :HEAD_DIM, :]
                                  * (1.0 / acc_ref[c, 2 * pr + e, HEAD_DIM:HEAD_DIM + 1, :]) for e in range(2)], axis=0)
            cols = slice(pr * LANES, (pr + 1) * LANES)
            o_ref[rows, cols] = (ot.T * _silu(z_ref[rows, cols].astype(F32))).astype(BF16)
        return carry

    lax.fori_loop(0, n_blocks, finish, 0)


def _fox_attention(proj, kfeat, qfeat_t, seq):
    t = proj.shape[0]
    b = t // seq
    nblk = seq // FOX_BLK
    w = FOX_PAIRS * LANES
    cb = lambda off: off // w
    tq, tj = _fox_schedule(nblk)
    col = lambda off: pl.BlockSpec((seq, w), lambda bi, p: (bi, cb(off) + p))
    return pl.pallas_call(
        _fox_kernel,
        grid=(b, N_PAIRS // FOX_PAIRS),
        in_specs=[
            pl.BlockSpec(memory_space=pltpu.SMEM), pl.BlockSpec(memory_space=pltpu.SMEM),
            col(COL_C_Q), col(COL_C_K), col(COL_C_V),
            pl.BlockSpec((seq, w), lambda bi, p: (bi, p)),
            pl.BlockSpec((None, seq // CHUNK, 16 * FOX_PAIRS, CHUNK), lambda bi, p: (bi, 0, p, 0)),
            col(COL_C_Z),
        ],
        out_specs=pl.BlockSpec((seq, w), lambda bi, p: (bi, p)),
        out_shape=jax.ShapeDtypeStruct((t, D_MODEL), BF16),
        scratch_shapes=[
            pltpu.VMEM((FOX_HEADS, seq, LANES), BF16),
            pltpu.VMEM((FOX_HEADS, nblk, FOX_VROWS, FOX_BLK), BF16),
            pltpu.VMEM((nblk, FOX_HEADS, LANES, FOX_BLK), BF16),
            pltpu.VMEM((nblk, FOX_HEADS, FOX_VROWS, FOX_BLK), F32),
            pltpu.VMEM((nblk, FOX_HEADS, 1, FOX_BLK), F32),
            pltpu.VMEM((2, FOX_HEADS, 1, FOX_BLK), F32),
            pltpu.VMEM((2, FOX_HEADS, 1, FOX_BLK), F32),
            pltpu.VMEM((2, FOX_HEADS, FOX_BLK, FOX_BLK), F32),
            pltpu.VMEM((2, FOX_HEADS, FOX_BLK, FOX_BLK), BF16),
        ],
        compiler_params=_cparams(2),
        name="fox_attn",
    )(jnp.asarray(tq), jnp.asarray(tj), proj, proj, proj, kfeat, qfeat_t, proj)


SWA_SUB = 1


def _rope(x, cos, sin_signed, first_half):
    partner = jnp.where(first_half, pltpu.roll(x, 96, 1), pltpu.roll(x, 32, 1))
    return x * cos + partner * sin_signed


def _swa_kernel(sinks_ref, q_ref, kc_ref, kp_ref, vc_ref, vp_ref, z_ref, cosc_ref, sinc_ref, cosp_ref, sinp_ref,
                o_ref):
    i = pl.program_id(1)
    blk = CHUNK
    lane = lax.broadcasted_iota(jnp.int32, (blk, LANES), 1)
    first_half = (lane % HEAD_DIM) < (HEAD_DIM // 2)
    even_head = lane < HEAD_DIM

    qidx = lax.broadcasted_iota(jnp.int32, (blk, 2 * blk), 0)
    sidx = lax.broadcasted_iota(jnp.int32, (blk, 2 * blk), 1)
    band = (sidx > qidx) & (sidx <= qidx + blk)
    first_key = jnp.where(i > 0, 0, blk)
    band_first = band & (sidx >= first_key)

    for kv in range(SWA_KV_HEADS):
        cols = slice(kv * LANES, (kv + 1) * LANES)
        ks = [_rope(kp_ref[:, cols].astype(F32), cosp_ref[...], sinp_ref[...], first_half).astype(BF16)]
        vs = [vp_ref[:, cols]]
        for s in range(SWA_SUB):
            rows = slice(s * blk, (s + 1) * blk)
            ks.append(_rope(kc_ref[rows, cols].astype(F32), cosc_ref[rows, :], sinc_ref[rows, :],
                            first_half).astype(BF16))
            vs.append(vc_ref[rows, cols])
        for s in range(SWA_SUB):
            rows = slice(s * blk, (s + 1) * blk)
            k_all = jnp.concatenate([ks[s], ks[s + 1]], axis=0)
            v_all = jnp.concatenate([vs[s], vs[s + 1]], axis=0)
            keep = band_first if s == 0 else band
            for pp in range(2):
                pair = 2 * kv + pp
                pcols = slice(pair * LANES, (pair + 1) * LANES)
                qr = _rope(q_ref[rows, pcols].astype(F32), cosc_ref[rows, :], sinc_ref[rows, :], first_half)
                outs = []
                for e in range(2):
                    sel = even_head if e == 0 else jnp.logical_not(even_head)
                    qh = jnp.where(sel, qr, 0.0).astype(BF16)
                    sc = jnp.where(keep, _dot_nt(qh, k_all), NEG)
                    sink = sinks_ref[2 * pair + e] * LOG2E
                    m = jnp.maximum(jnp.max(sc, axis=-1, keepdims=True), sink)
                    p = jnp.exp2(sc - m)
                    denom = jnp.sum(p, axis=-1, keepdims=True) + jnp.exp2(sink - m)
                    outs.append(_dot(p.astype(BF16), v_all) * (1.0 / denom))
                y = jnp.where(even_head, outs[0], outs[1])
                o_ref[rows, pcols] = (y * _silu(z_ref[rows, pcols].astype(F32))).astype(BF16)


def _swa_attention(proj, sinks, cos_t, sin_t, seq):
    t = proj.shape[0]
    b = t // seq
    nblk = seq // CHUNK
    tm = SWA_SUB * CHUNK
    nstep = seq // tm
    kvw = SWA_KV_HEADS * LANES
    prev_blk = lambda i: jnp.maximum(SWA_SUB * i - 1, 0)
    cur = lambda w, off: pl.BlockSpec((tm, w), lambda bi, i: (bi * nstep + i, off // w))
    prev = lambda w, off: pl.BlockSpec((CHUNK, w), lambda bi, i: (bi * nblk + prev_blk(i), off // w))
    return pl.pallas_call(
        _swa_kernel,
        grid=(b, nstep),
        in_specs=[
            pl.BlockSpec(memory_space=pltpu.SMEM),
            cur(D_MODEL, COL_B_Q),
            cur(kvw, COL_B_K), prev(kvw, COL_B_K),
            cur(kvw, COL_B_V), prev(kvw, COL_B_V),
            cur(D_MODEL, COL_B_Z),
            pl.BlockSpec((tm, LANES), lambda bi, i: (i, 0)),
            pl.BlockSpec((tm, LANES), lambda bi, i: (i, 0)),
            pl.BlockSpec((CHUNK, LANES), lambda bi, i: (prev_blk(i), 0)),
            pl.BlockSpec((CHUNK, LANES), lambda bi, i: (prev_blk(i), 0)),
        ],
        out_specs=pl.BlockSpec((tm, D_MODEL), lambda bi, i: (bi * nstep + i, 0)),
        out_shape=jax.ShapeDtypeStruct((t, D_MODEL), BF16),
        compiler_params=_cparams(2),
        name="swa_attn",
    )(sinks, proj, proj, proj, proj, proj, proj, cos_t, sin_t, cos_t, sin_t)


MAMBA_SUB = 4


def _mamba_kernel(xbc_ref, z_ref, sm_ref, cw_ref, cb_ref, dtb_ref, alog_ref, dsk_ref, nw_ref, o_ref,
                  prev_ref, state_ref):
    @pl.when(pl.program_id(1) == 0)
    def _():
        prev_ref[...] = jnp.zeros_like(prev_ref)
        state_ref[...] = jnp.zeros_like(state_ref)

    prev = prev_ref[...]
    for sub in range(MAMBA_SUB):
        rows = slice(sub * CHUNK, (sub + 1) * CHUNK)
        _mamba_chunk(xbc_ref, z_ref, sm_ref, cw_ref, cb_ref, dtb_ref, alog_ref, dsk_ref, nw_ref, o_ref, state_ref,
                     rows, prev)
        prev = xbc_ref[rows, :]
    prev_ref[...] = prev


def _mamba_chunk(xbc_ref, z_ref, sm_ref, cw_ref, cb_ref, dtb_ref, alog_ref, dsk_ref, nw_ref, o_ref, state_ref,
                 rows, prev):
    n = CHUNK
    cur = xbc_ref[rows, :]
    both = jnp.concatenate([prev, cur], axis=0)
    srow = lax.broadcasted_iota(jnp.int32, (n, 2 * n), 0)
    scol = lax.broadcasted_iota(jnp.int32, (n, 2 * n), 1)
    shifts = jnp.concatenate([(scol == srow + (n - (CONV_WIDTH - 1) + k)).astype(BF16)
                              for k in range(CONV_WIDTH - 1)], axis=0)
    shifted = _dot(shifts, both)
    conv = cb_ref[...] + cw_ref[CONV_WIDTH - 1:CONV_WIDTH, :] * cur.astype(F32)
    for k in range(CONV_WIDTH - 1):
        conv = conv + cw_ref[k:k + 1, :] * shifted[k * n:(k + 1) * n, :]
    xbc = _silu(conv)
    xs = xbc[:, :D_MODEL]
    bm = xbc[:, D_MODEL:D_MODEL + SSM_GROUPS * SSM_STATE]
    cm = xbc[:, D_MODEL + SSM_GROUPS * SSM_STATE:]
    xs_b = xs.astype(BF16)

    dt = _softplus(sm_ref[rows, :] + dtb_ref[...])
    a_dt = dt * (jnp.exp(alog_ref[...]) * (-LOG2E))
    row = lax.broadcasted_iota(jnp.int32, (n, n), 0)
    col = lax.broadcasted_iota(jnp.int32, (n, n), 1)
    causal = row >= col
    tri = causal.astype(BF16)
    loc = _dot(tri, jnp.concatenate(_split3(a_dt), axis=1))
    acum = loc[:, :LANES] + loc[:, LANES:2 * LANES] + loc[:, 2 * LANES:]
    acum_t = acum.T
    dt_t = dt.T
    eacum = jnp.exp2(acum)
    lane = lax.broadcasted_iota(jnp.int32, (n, LANES), 1)
    even_head = lane < HEAD_DIM

    ys = []
    for g in range(SSM_GROUPS):
        gcols = slice(g * SSM_STATE, (g + 1) * SSM_STATE)
        bg = bm[:, gcols]
        cg = cm[:, gcols]
        cbm = _dot_nt(cg.astype(BF16), bg.astype(BF16))
        bg_t = bg.T
        for pp in range(2):
            pair = 2 * g + pp
            pcols = slice(pair * LANES, (pair + 1) * LANES)
            st_old = state_ref[:, pcols]
            rhs = jnp.concatenate([xs_b[:, pcols], st_old.astype(BF16)], axis=0)
            outs, news, sdec = [], [], []
            for e in range(2):
                h = 2 * pair + e
                a_col = acum[:, h:h + 1]
                a_row = acum_t[h:h + 1, :]
                dt_row = dt_t[h:h + 1, :]
                decay = jnp.exp2(jnp.where(causal, a_col - a_row, NEG))
                m_in = (cbm * decay * dt_row).astype(BF16)
                c_off = (cg * eacum[:, h:h + 1]).astype(BF16)
                outs.append(_dot(jnp.concatenate([m_in, c_off], axis=1), rhs))
                a_last = a_row[:, n - 1:n]
                w_row = jnp.exp2(a_last - a_row) * dt_row
                news.append(_dot((bg_t * w_row).astype(BF16), xs_b[:, pcols]))
                sdec.append(jnp.exp2(a_last))
            ys.append(jnp.where(even_head, outs[0], outs[1]))
            state_ref[:, pcols] = (st_old * jnp.where(even_head, sdec[0], sdec[1])
                                   + jnp.where(even_head, news[0], news[1]))

    y = jnp.concatenate(ys, axis=1) + dsk_ref[...] * xs
    y = y * _silu(z_ref[rows, :].astype(F32))
    gw = D_MODEL // SSM_GROUPS
    for g in range(SSM_GROUPS):
        gc = slice(g * gw, (g + 1) * gw)
        yg = y[:, gc]
        ms = jnp.mean(yg * yg, axis=-1, keepdims=True)
        o_ref[rows, gc] = (yg * lax.rsqrt(ms + NORM_EPS) * nw_ref[:, gc]).astype(BF16)


def _mamba(proj, small, conv_w, conv_b, dtb_row, alog_row, dskip_full, norm_w, seq):
    t = proj.shape[0]
    b = t // seq
    tm = MAMBA_SUB * CHUNK
    nchunk = seq // tm
    xw = 2 * D_MODEL
    const = lambda shape: pl.BlockSpec(shape, lambda bi, c: (0,) * len(shape))
    return pl.pallas_call(
        _mamba_kernel,
        grid=(b, nchunk),
        in_specs=[
            pl.BlockSpec((tm, xw), lambda bi, c: (bi * nchunk + c, COL_A_XBC // xw)),
            pl.BlockSpec((tm, D_MODEL), lambda bi, c: (bi * nchunk + c, COL_A_Z // D_MODEL)),
            pl.BlockSpec((tm, LANES), lambda bi, c: (bi * nchunk + c, 0)),
            const((CONV_WIDTH, xw)), const((1, xw)), const((1, LANES)), const((1, LANES)),
            const((1, D_MODEL)), const((1, D_MODEL)),
        ],
        out_specs=pl.BlockSpec((tm, D_MODEL), lambda bi, c: (bi * nchunk + c, 0)),
        out_shape=jax.ShapeDtypeStruct((t, D_MODEL), BF16),
        scratch_shapes=[
            pltpu.VMEM((CHUNK, xw), BF16),
            pltpu.VMEM((SSM_STATE, D_MODEL), F32),
        ],
        compiler_params=_cparams(2),
        name="mamba",
    )(proj, proj, small, conv_w, conv_b, dtb_row, alog_row, dskip_full, norm_w)


OUT_TM = 512


def _merge_kernel(x_ref, ya_ref, yb_ref, yc_ref, g_ref, gb_ref, wp_ref, wo_ref, fnw_ref, o_ref, *, final_norm):
    merged = None
    for i, y_ref in enumerate((ya_ref, yb_ref, yc_ref)):
        branch = _dot(y_ref[...], wp_ref[i])
        gcols = slice(i * D_MODEL, (i + 1) * D_MODEL)
        gate = 1.0 / (1.0 + jnp.exp2((g_ref[:, gcols].astype(F32) + gb_ref[:, gcols]) * (-LOG2E)))
        merged = gate * branch if merged is None else merged + gate * branch
    x = x_ref[...] + _dot(merged.astype(BF16), wo_ref[...])
    if final_norm:
        ms = jnp.mean(x * x, axis=-1, keepdims=True)
        x = x * lax.rsqrt(ms + NORM_EPS) * fnw_ref[...]
    o_ref[...] = x


def _merge(x2, ya, yb, yc, proj, gate_bias, w_proj, w_out, final_norm_w, final_norm):
    t = x2.shape[0]
    gw = 3 * D_MODEL
    row = lambda w, cbi: pl.BlockSpec((OUT_TM, w), lambda i: (i, cbi))
    const = lambda shape: pl.BlockSpec(shape, lambda i: (0,) * len(shape))
    return pl.pallas_call(
        functools.partial(_merge_kernel, final_norm=final_norm),
        grid=(t // OUT_TM,),
        in_specs=[
            row(D_MODEL, 0), row(D_MODEL, 0), row(D_MODEL, 0), row(D_MODEL, 0),
            row(gw, COL_GATES // gw),
            const((1, gw)), const((3, D_MODEL, D_MODEL)), const((D_MODEL, D_MODEL)), const((1, D_MODEL)),
        ],
        out_specs=row(D_MODEL, 0),
        out_shape=jax.ShapeDtypeStruct((t, D_MODEL), F32),
        compiler_params=_cparams(1),
        name="merge_out",
    )(x2, ya, yb, yc, proj, gate_bias, w_proj, w_out, final_norm_w)


def _dup_heads(w):
    d = w.shape[0]
    w4 = w.reshape(d, SWA_KV_HEADS, HEAD_DIM)
    return jnp.concatenate([w4, w4], axis=-1).reshape(d, SWA_KV_HEADS * LANES)


def _layout_w_in(w):
    offs = np.cumsum((0,) + _IN_SIZES)
    (a_xbc, a_z, a_dt, b_q, b_k, b_v, b_z, c_q, c_k, c_v, c_f, c_z, gates) = [
        w[:, offs[i]:offs[i + 1]] for i in range(len(_IN_SIZES))]
    scale = HEAD_DIM ** -0.5
    big = jnp.concatenate([a_xbc, a_z, b_z, c_z, b_q * (scale * LOG2E), c_q * (scale * LOG2E), c_k, c_v, gates,
                           _dup_heads(b_k), _dup_heads(b_v)], axis=1).astype(BF16)
    small = jnp.concatenate([a_dt, c_f, jnp.zeros((w.shape[0], LANES - 32), w.dtype)], axis=1)
    s_hi = small.astype(BF16)
    s_lo = (small - s_hi.astype(F32)).astype(BF16)
    return big, jnp.stack([s_hi, s_lo])


def _lane_row(v, start):
    return jnp.zeros((1, LANES), F32).at[0, start:start + N_HEADS].set(v.astype(F32))


def _rope_tables(seq):
    pos = jnp.arange(seq, dtype=F32)
    inv_freq = ROPE_THETA ** (-jnp.arange(0, HEAD_DIM, 2, dtype=F32) / HEAD_DIM)
    ang = pos[:, None] * inv_freq[None, :]
    cos, sin = jnp.cos(ang), jnp.sin(ang)
    cos_t = jnp.concatenate([cos, cos, cos, cos], axis=1)
    sin_t = jnp.concatenate([-sin, sin, -sin, sin], axis=1)
    return cos_t, sin_t


def kernel(x, norm_w, w_in, conv_w, conv_b, dt_bias, a_log, d_skip, ssm_norm_w,
           sinks, f_bias, gate_bias, w_proj, w_out, final_norm_w):
    b, s, d = x.shape
    depth = norm_w.shape[0]
    assert d == D_MODEL and s % FOX_BLK == 0 and s >= 2 * FOX_BLK and (b * s) % IN_TM == 0
    cos_t, sin_t = _rope_tables(s)
    x2 = x.reshape(b * s, d)
    for layer in range(depth):
        w_big, w_small = _layout_w_in(w_in[layer])
        proj, small = _inproj(x2, norm_w[layer][None, :], w_big, w_small)
        kfeat, qfeat_t = _fox_scan(small, _lane_row(f_bias[layer], LANE_F), s)
        ya = _mamba(proj, small, conv_w[layer], conv_b[layer][None, :], _lane_row(dt_bias[layer], 0),
                    _lane_row(a_log[layer], 0), jnp.repeat(d_skip[layer], HEAD_DIM)[None, :],
                    ssm_norm_w[layer][None, :], s)
        yb = _swa_attention(proj, sinks[layer], cos_t, sin_t, s)
        yc = _fox_attention(proj, kfeat, qfeat_t, s)
        x2 = _merge(x2, ya, yb, yc, proj, gate_bias[layer].reshape(1, 3 * d),
                    w_proj[layer].astype(BF16), w_out[layer].astype(BF16), final_norm_w[None, :],
                    final_norm=(layer == depth - 1))
    return x2.reshape(b, s, d)
```

```python
import functools
import math

import jax
import jax.numpy as jnp
import numpy as np
from jax import lax
from jax.experimental import pallas as pl
from jax.experimental.pallas import tpu as pltpu

F32 = jnp.float32
BF16 = jnp.bfloat16

D_MODEL = 1024
HEAD_DIM = 64
N_HEADS = 16
N_PAIRS = N_HEADS // 2
LANES = 128
SSM_GROUPS = 4
SSM_STATE = 128
CONV_WIDTH = 4
CHUNK = 128
SWA_KV_HEADS = 4
ROPE_THETA = 10000.0
NORM_EPS = 1e-6
NEG = -1e30
LOG2E = math.log2(math.e)

COL_A_XBC = 0
COL_A_Z = 2048
COL_B_Z = 3072
COL_C_Z = 4096
COL_B_Q = 5120
COL_C_Q = 6144
COL_C_K = 7168
COL_C_V = 8192
COL_GATES = 9216
COL_B_K = 12288
COL_B_V = 12800
N_PROJ = 13312

_IN_SIZES = (2048, 1024, 16, 1024, 256, 256, 1024, 1024, 1024, 1024, 16, 1024, 3072)

VMEM_LIMIT = 56 * 1024 * 1024


def _cparams(n_axes):
    return pltpu.CompilerParams(dimension_semantics=("arbitrary",) * n_axes, vmem_limit_bytes=VMEM_LIMIT)


def _split3(x):
    hi = x.astype(BF16)
    r1 = x - hi.astype(F32)
    mid = r1.astype(BF16)
    lo = (r1 - mid.astype(F32)).astype(BF16)
    return hi, mid, lo


def _dot(a, b):
    return jnp.dot(a, b, preferred_element_type=F32)


def _dot_nt(a, b):
    return lax.dot_general(a, b, (((1,), (1,)), ((), ())), preferred_element_type=F32)


def _silu(x):
    return x * (1.0 / (1.0 + jnp.exp2(x * (-LOG2E))))


def _softplus(x):
    return jnp.maximum(x, 0.0) + jnp.log1p(jnp.exp(-jnp.abs(x)))


IN_TM = 2048
IN_TN = 1024
IN_NORM_ROWS = 512


def _inproj_kernel(x_ref, nw_ref, w_ref, wsm_ref, o_ref, osm_ref, h_ref):
    @pl.when(pl.program_id(1) == 0)
    def _():
        for r in range(IN_TM // IN_NORM_ROWS):
            rows = slice(r * IN_NORM_ROWS, (r + 1) * IN_NORM_ROWS)
            x = x_ref[rows, :]
            ms = jnp.mean(x * x, axis=-1, keepdims=True)
            h = x * lax.rsqrt(ms + NORM_EPS) * nw_ref[...]
            hh = h.astype(BF16)
            h_ref[rows, :] = hh
            hl = (h - hh.astype(F32)).astype(BF16)
            osm_ref[rows, :] = _dot(hh, wsm_ref[0]) + _dot(hl, wsm_ref[0]) + _dot(hh, wsm_ref[1])

    o_ref[...] = _dot(h_ref[...], w_ref[...]).astype(BF16)


def _inproj(x2, norm_w, w_big, w_small):
    t = x2.shape[0]
    grid = (t // IN_TM, N_PROJ // IN_TN)
    return pl.pallas_call(
        _inproj_kernel,
        grid=grid,
        in_specs=[
            pl.BlockSpec((IN_TM, D_MODEL), lambda i, j: (i, 0)),
            pl.BlockSpec((1, D_MODEL), lambda i, j: (0, 0)),
            pl.BlockSpec((D_MODEL, IN_TN), lambda i, j: (0, j)),
            pl.BlockSpec((2, D_MODEL, LANES), lambda i, j: (0, 0, 0)),
        ],
        out_specs=[
            pl.BlockSpec((IN_TM, IN_TN), lambda i, j: (i, j)),
            pl.BlockSpec((IN_TM, LANES), lambda i, j: (i, 0)),
        ],
        out_shape=[
            jax.ShapeDtypeStruct((t, N_PROJ), BF16),
            jax.ShapeDtypeStruct((t, LANES), F32),
        ],
        scratch_shapes=[pltpu.VMEM((IN_TM, D_MODEL), BF16)],
        compiler_params=_cparams(2),
        name="inproj",
    )(x2, norm_w, w_big, w_small)


LANE_F = 16


def _scan_selectors():
    selk = np.zeros((3 * LANES, N_PAIRS * LANES), np.float32)
    konst = np.zeros((1, N_PAIRS * LANES), np.float32)
    selq = np.zeros((LANES, 3 * LANES), np.float32)
    qonst = np.zeros((LANES, 1), np.float32)
    for p in range(N_PAIRS):
        for e in range(2):
            src = LANE_F + 2 * p + e
            kbase = p * LANES + (64 if e == 0 else 0)
            qbase = 16 * p + 8 * e
            for t in range(3):
                konst[0, kbase + t] = 1.0
                selk[t * LANES + src, kbase + 3 + t] = 1.0
                selq[qbase + t, t * LANES + src] = 1.0
                qonst[qbase + 3 + t, 0] = -1.0
    return selk, konst, selq, qonst


SCAN_UNROLL = 4


def _exact_prefix(mask_bf16, x):
    y = _dot(mask_bf16, jnp.concatenate(_split3(x), axis=1))
    return y[:, :LANES] + y[:, LANES:2 * LANES] + y[:, 2 * LANES:]


def _scan_kernel(sm_ref, fb_ref, selk_ref, konst_ref, selq_ref, qonst_ref, kf_ref, qf_ref, loc_ref, tot_ref):
    nchunk = sm_ref.shape[0] // CHUNK
    assert nchunk % SCAN_UNROLL == 0
    row = lax.broadcasted_iota(jnp.int32, (CHUNK, CHUNK), 0)
    col = lax.broadcasted_iota(jnp.int32, (CHUNK, CHUNK), 1)
    tri = (row >= col).astype(BF16)
    fb = fb_ref[...]
    chunk_rows = lambda c: pl.ds(pl.multiple_of(c * CHUNK, CHUNK), CHUNK)

    def unrolled(body):
        def trip(g, carry):
            for i in range(SCAN_UNROLL):
                body(g * SCAN_UNROLL + i)
            return carry
        lax.fori_loop(0, nchunk // SCAN_UNROLL, trip, 0)

    def local_scan(c):
        v = sm_ref[chunk_rows(c), :] + fb
        lf = (jnp.minimum(v, 0.0) - jnp.log1p(jnp.exp(-jnp.abs(v)))) * LOG2E
        loc = _exact_prefix(tri, lf)
        loc_ref[chunk_rows(c), :] = loc
        tot_ref[pl.ds(c, 1), :] = loc[CHUNK - 1:CHUNK, :]

    unrolled(local_scan)

    crow = lax.broadcasted_iota(jnp.int32, (nchunk, nchunk), 0)
    ccol = lax.broadcasted_iota(jnp.int32, (nchunk, nchunk), 1)
    tot_ref[...] = _exact_prefix((crow > ccol).astype(BF16), tot_ref[...])

    def emit(c):
        cum = loc_ref[chunk_rows(c), :] + tot_ref[pl.ds(c, 1), :]
        c3 = jnp.concatenate(_split3(cum), axis=1)
        kf_ref[chunk_rows(c), :] = (_dot(c3, selk_ref[...]) + konst_ref[...]).astype(BF16)
        qf_ref[c] = _dot_nt(selq_ref[...], c3) + qonst_ref[...]

    unrolled(emit)


def _fox_scan(small, fb_row, seq):
    t = small.shape[0]
    b = t // seq
    selk, konst, selq, qonst = _scan_selectors()
    const = lambda shape: pl.BlockSpec(shape, lambda i: (0,) * len(shape))
    return pl.pallas_call(
        _scan_kernel,
        grid=(b,),
        in_specs=[
            pl.BlockSpec((seq, LANES), lambda i: (i, 0)),
            const((1, LANES)),
            const(selk.shape), const(konst.shape), const(selq.shape), const(qonst.shape),
        ],
        out_specs=[
            pl.BlockSpec((seq, N_PAIRS * LANES), lambda i: (i, 0)),
            pl.BlockSpec((None, seq // CHUNK, LANES, CHUNK), lambda i: (i, 0, 0, 0)),
        ],
        out_shape=[
            jax.ShapeDtypeStruct((t, N_PAIRS * LANES), BF16),
            jax.ShapeDtypeStruct((b, seq // CHUNK, LANES, CHUNK), F32),
        ],
        scratch_shapes=[
            pltpu.VMEM((seq, LANES), F32),
            pltpu.VMEM((seq // CHUNK, LANES), F32),
        ],
        compiler_params=_cparams(1),
        name="fox_scan",
    )(small, fb_row, jnp.asarray(selk, BF16), jnp.asarray(konst, F32), jnp.asarray(selq, BF16),
      jnp.asarray(qonst, F32))


FOX_BLK = 256
FOX_PAIRS = 2
FOX_HEADS = 2 * FOX_PAIRS
FOX_UNROLL = 4
FOX_VROWS = HEAD_DIM + 16


def _fox_schedule(n_blocks):
    items = [(q, q) for q in range(n_blocks)] + [(q, j) for q in range(n_blocks) for j in range(q)]
    return np.asarray([it[0] for it in items], np.int32), np.asarray([it[1] for it in items], np.int32)


def _fox_kernel(tq_ref, tj_ref, q_ref, k_ref, v_ref, kf_ref, qf_ref, z_ref, o_ref,
                    kaug_ref, vt_ref, qt_ref, acc_ref, m_ref, a_ref, mx_ref, st_ref, p_ref):
    seq = k_ref.shape[0]
    nb = FOX_BLK
    n_blocks = seq // nb
    n_items = n_blocks * (n_blocks + 1) // 2
    blk_rows = lambda i: pl.ds(pl.multiple_of(i * nb, nb), nb)

    lane = lax.broadcasted_iota(jnp.int32, (nb, LANES), 1)
    ones = jnp.ones((FOX_VROWS - HEAD_DIM, nb), BF16)
    zeros56 = jnp.zeros((HEAD_DIM - 8, nb), F32)

    def setup(c, carry):
        rows = blk_rows(c)
        for pr in range(FOX_PAIRS):
            cols = slice(pr * LANES, (pr + 1) * LANES)
            k2 = k_ref[rows, cols]
            kf = kf_ref[rows, cols]
            kaug_ref[2 * pr, rows, :] = jnp.where(lane < HEAD_DIM, k2, kf)
            kaug_ref[2 * pr + 1, rows, :] = jnp.where(lane >= HEAD_DIM, k2, kf)
            vt = v_ref[rows, cols].astype(F32).T.astype(BF16)
            vt_ref[2 * pr, c] = jnp.concatenate([vt[:HEAD_DIM], ones], axis=0)
            vt_ref[2 * pr + 1, c] = jnp.concatenate([vt[HEAD_DIM:], ones], axis=0)
            qt = q_ref[rows, cols].astype(F32).T
            qf = jnp.concatenate([qf_ref[(nb // CHUNK) * c + s, 16 * pr:16 * (pr + 1), :]
                                  for s in range(nb // CHUNK)], axis=1)
            qt_ref[c, 2 * pr] = jnp.concatenate([qt[:HEAD_DIM], qf[:8], zeros56], axis=0).astype(BF16)
            qt_ref[c, 2 * pr + 1] = jnp.concatenate([qf[8:], zeros56, qt[HEAD_DIM:]], axis=0).astype(BF16)
        return carry

    def pairwise(body):
        assert n_blocks % 2 == 0
        lax.fori_loop(0, n_blocks // 2, lambda g, carry: body(2 * g + 1, body(2 * g, carry)), 0)

    pairwise(setup)
    acc_ref[...] = jnp.zeros_like(acc_ref)
    m_ref[...] = jnp.full(m_ref.shape, NEG, F32)

    def scores(u, slot):
        qb, kb = tq_ref[u], tj_ref[u]
        for h in range(FOX_HEADS):
            st = _dot(kaug_ref[h, blk_rows(kb), :], qt_ref[qb, h])
            st_ref[slot, h] = st
            mx_ref[slot, h] = jnp.max(st, axis=0, keepdims=True)

    def softmax(u, slot, masked):
        qb = tq_ref[u]
        if masked:
            keep = (lax.broadcasted_iota(jnp.int32, (nb, nb), 0) <= lax.broadcasted_iota(jnp.int32, (nb, nb), 1))
        for h in range(FOX_HEADS):
            st = st_ref[slot, h]
            if masked:
                st = jnp.where(keep, st, NEG)
                blk_max = jnp.max(st, axis=0, keepdims=True)
            else:
                blk_max = mx_ref[slot, h]
            m_old = m_ref[qb, h]
            m_new = jnp.maximum(m_old, blk_max)
            p_ref[slot, h] = jnp.exp2(st - m_new).astype(BF16)
            a_ref[slot, h] = jnp.exp2(m_old - m_new)
            m_ref[qb, h] = m_new

    def values(u, slot):
        qb, kb = tq_ref[u], tj_ref[u]
        for h in range(FOX_HEADS):
            acc_ref[qb, h] = a_ref[slot, h] * acc_ref[qb, h] + _dot(vt_ref[h, kb], p_ref[slot, h])

    def step(u, slot, masked):
        values(u - 2, slot)
        softmax(u - 1, 1 - slot, masked)
        scores(u, slot)

    def run(lo, hi, masked):
        while lo < hi and lo % FOX_UNROLL:
            step(lo, lo % 2, masked)
            lo += 1
        trips = (hi - lo) // FOX_UNROLL
        if trips:
            def trip(g, carry):
                for i in range(FOX_UNROLL):
                    step(lo + g * FOX_UNROLL + i, i % 2, masked)
                return carry
            lax.fori_loop(0, trips, trip, 0)
        for u in range(lo + trips * FOX_UNROLL, hi):
            step(u, u % 2, masked)

    scores(0, 0)
    scores(1, 1)
    softmax(0, 0, True)
    run(2, n_blocks + 1, True)
    run(n_blocks + 1, n_items, False)
    values(n_items - 2, n_items % 2)
    softmax(n_items - 1, (n_items - 1) % 2, False)
    values(n_items - 1, (n_items - 1) % 2)

    def finish(c, carry):
        rows = blk_rows(c)
        for pr in range(FOX_PAIRS):
            ot = jnp.concatenate([acc_ref[c, 2 * pr + e, :HEAD_DIM, :]
                                  * (1.0 / acc_ref[c, 2 * pr + e, HEAD_DIM:HEAD_DIM + 1, :]) for e in range(2)], axis=0)
            cols = slice(pr * LANES, (pr + 1) * LANES)
            o_ref[rows, cols] = (ot.T * _silu(z_ref[rows, cols].astype(F32))).astype(BF16)
        return carry

    pairwise(finish)


def _fox_attention(proj, kfeat, qfeat_t, seq):
    t = proj.shape[0]
    b = t // seq
    nblk = seq // FOX_BLK
    w = FOX_PAIRS * LANES
    cb = lambda off: off // w
    tq, tj = _fox_schedule(nblk)
    col = lambda off: pl.BlockSpec((seq, w), lambda bi, p: (bi, cb(off) + p))
    return pl.pallas_call(
        _fox_kernel,
        grid=(b, N_PAIRS // FOX_PAIRS),
        in_specs=[
            pl.BlockSpec(memory_space=pltpu.SMEM), pl.BlockSpec(memory_space=pltpu.SMEM),
            col(COL_C_Q), col(COL_C_K), col(COL_C_V),
            pl.BlockSpec((seq, w), lambda bi, p: (bi, p)),
            pl.BlockSpec((None, seq // CHUNK, 16 * FOX_PAIRS, CHUNK), lambda bi, p: (bi, 0, p, 0)),
            col(COL_C_Z),
        ],
        out_specs=pl.BlockSpec((seq, w), lambda bi, p: (bi, p)),
        out_shape=jax.ShapeDtypeStruct((t, D_MODEL), BF16),
        scratch_shapes=[
            pltpu.VMEM((FOX_HEADS, seq, LANES), BF16),
            pltpu.VMEM((FOX_HEADS, nblk, FOX_VROWS, FOX_BLK), BF16),
            pltpu.VMEM((nblk, FOX_HEADS, LANES, FOX_BLK), BF16),
            pltpu.VMEM((nblk, FOX_HEADS, FOX_VROWS, FOX_BLK), F32),
            pltpu.VMEM((nblk, FOX_HEADS, 1, FOX_BLK), F32),
            pltpu.VMEM((2, FOX_HEADS, 1, FOX_BLK), F32),
            pltpu.VMEM((2, FOX_HEADS, 1, FOX_BLK), F32),
            pltpu.VMEM((2, FOX_HEADS, FOX_BLK, FOX_BLK), F32),
            pltpu.VMEM((2, FOX_HEADS, FOX_BLK, FOX_BLK), BF16),
        ],
        compiler_params=_cparams(2),
        name="fox_attn",
    )(jnp.asarray(tq), jnp.asarray(tj), proj, proj, proj, kfeat, qfeat_t, proj)


SWT_Q = 256
SWT_WIN = SWT_Q + CHUNK
SWT_HEADS = 4
SWT_VROWS = HEAD_DIM + 16


def _rope(x, cos, sin_signed, first_half):
    partner = jnp.where(first_half, pltpu.roll(x, 96, 1), pltpu.roll(x, 32, 1))
    return x * cos + partner * sin_signed


def _swt_kernel(sinks_ref, q_ref, k_ref, v_ref, z_ref, cos_ref, sin_ref, cost_ref, sint_ref, o_ref,
                kaug_ref, vt_ref, qt_ref, st_ref, p_ref, e_ref):
    g = pl.program_id(1)
    seq = k_ref.shape[0]
    nq = SWT_Q
    n_items = seq // nq
    lane = lax.broadcasted_iota(jnp.int32, (nq, LANES), 1)
    first_half = (lane % HEAD_DIM) < (HEAD_DIM // 2)
    even_head = lane < HEAD_DIM
    q_rows = lambda c: pl.ds(pl.multiple_of(c * nq, nq), nq)
    zero_k = jnp.zeros((nq, LANES), BF16)
    ones = jnp.ones((SWT_VROWS - HEAD_DIM, CHUNK), BF16)

    kaug_ref[:, 0:CHUNK, :] = jnp.zeros((2, CHUNK, LANES), BF16)
    vt_ref[0] = jnp.zeros((SWT_VROWS, CHUNK), BF16)

    def setup(c, carry):
        rows = q_rows(c)
        cos, sin = cos_ref[rows, :], sin_ref[rows, :]
        kr = _rope(k_ref[rows, :].astype(F32), cos, sin, first_half).astype(BF16)
        dst = pl.ds(pl.multiple_of(c * nq + CHUNK, CHUNK), nq)
        kaug_ref[0, dst, :] = jnp.where(even_head, kr, zero_k)
        kaug_ref[1, dst, :] = jnp.where(even_head, zero_k, kr)
        vt = v_ref[rows, :].astype(F32).T.astype(BF16)
        for s in range(nq // CHUNK):
            vt_ref[(nq // CHUNK) * c + 1 + s] = jnp.concatenate(
                [vt[:HEAD_DIM, s * CHUNK:(s + 1) * CHUNK], ones], axis=0)
        half = HEAD_DIM // 2
        for pr in range(SWT_HEADS // 2):
            cols = slice(pr * LANES, (pr + 1) * LANES)
            qt = q_ref[rows, cols].astype(F32).T
            partner = jnp.concatenate([qt[half:2 * half], qt[:half], qt[3 * half:], qt[2 * half:3 * half]], axis=0)
            qt_ref[c, pr] = (qt * cost_ref[c] + partner * sint_ref[c]).astype(BF16)
        return carry

    lax.fori_loop(0, n_items // 2, lambda i, carry: setup(2 * i + 1, setup(2 * i, carry)), 0)

    kidx = lax.broadcasted_iota(jnp.int32, (SWT_WIN, nq), 0)
    qidx = lax.broadcasted_iota(jnp.int32, (SWT_WIN, nq), 1)

    def scores(c, slot):
        win = pl.ds(pl.multiple_of(c * nq, nq), SWT_WIN)
        for h in range(SWT_HEADS):
            st_ref[slot, h] = _dot(kaug_ref[h % 2, win, :], qt_ref[c, h // 2])

    def softmax(slot, first_item):
        band = (kidx > qidx) & (kidx <= qidx + CHUNK)
        keep = band & (kidx >= CHUNK) if first_item else band
        for h in range(SWT_HEADS):
            sink = sinks_ref[SWT_HEADS * g + h] * LOG2E
            st = jnp.where(keep, st_ref[slot, h], NEG)
            m = jnp.maximum(jnp.max(st, axis=0, keepdims=True), sink)
            p_ref[slot, h] = jnp.exp2(st - m).astype(BF16)
            e_ref[slot, h] = jnp.exp2(sink - m)

    def values(c, slot):
        vth = jnp.concatenate([vt_ref[(nq // CHUNK) * c + s] for s in range(SWT_WIN // CHUNK)], axis=1)
        rows = q_rows(c)
        for pr in range(SWT_HEADS // 2):
            halves = []
            for e in range(2):
                h = 2 * pr + e
                o = _dot(vth, p_ref[slot, h])
                halves.append(o[:HEAD_DIM] * (1.0 / (o[HEAD_DIM:HEAD_DIM + 1] + e_ref[slot, h])))
            cols = slice(pr * LANES, (pr + 1) * LANES)
            y = jnp.concatenate(halves, axis=0).T
            o_ref[rows, cols] = (y * _silu(z_ref[rows, cols].astype(F32))).astype(BF16)

    def step(u, slot):
        values(u - 2, slot)
        softmax(1 - slot, False)
        scores(u, slot)

    scores(0, 0)
    scores(1, 1)
    softmax(0, True)
    assert n_items % 2 == 0 and n_items >= 4

    def trip(i, carry):
        step(2 + 2 * i, 0)
        step(3 + 2 * i, 1)
        return carry

    lax.fori_loop(0, (n_items - 2) // 2, trip, 0)
    values(n_items - 2, 0)
    softmax(1, False)
    values(n_items - 1, 1)


def _swt_attention(proj, sinks, cos_t, sin_t, seq):
    t = proj.shape[0]
    b = t // seq
    w = (SWT_HEADS // 2) * LANES
    n_items = seq // SWT_Q
    by_item = lambda tbl: tbl.T.reshape(LANES, n_items, SWT_Q).transpose(1, 0, 2)
    return pl.pallas_call(
        _swt_kernel,
        grid=(b, SWA_KV_HEADS),
        in_specs=[
            pl.BlockSpec(memory_space=pltpu.SMEM),
            pl.BlockSpec((seq, w), lambda bi, g: (bi, COL_B_Q // w + g)),
            pl.BlockSpec((seq, LANES), lambda bi, g: (bi, COL_B_K // LANES + g)),
            pl.BlockSpec((seq, LANES), lambda bi, g: (bi, COL_B_V // LANES + g)),
            pl.BlockSpec((seq, w), lambda bi, g: (bi, COL_B_Z // w + g)),
            pl.BlockSpec((seq, LANES), lambda bi, g: (0, 0)),
            pl.BlockSpec((seq, LANES), lambda bi, g: (0, 0)),
            pl.BlockSpec((n_items, LANES, SWT_Q), lambda bi, g: (0, 0, 0)),
            pl.BlockSpec((n_items, LANES, SWT_Q), lambda bi, g: (0, 0, 0)),
        ],
        out_specs=pl.BlockSpec((seq, w), lambda bi, g: (bi, g)),
        out_shape=jax.ShapeDtypeStruct((t, D_MODEL), BF16),
        scratch_shapes=[
            pltpu.VMEM((2, seq + CHUNK, LANES), BF16),
            pltpu.VMEM((seq // CHUNK + 1, SWT_VROWS, CHUNK), BF16),
            pltpu.VMEM((n_items, SWT_HEADS // 2, LANES, SWT_Q), BF16),
            pltpu.VMEM((2, SWT_HEADS, SWT_WIN, SWT_Q), F32),
            pltpu.VMEM((2, SWT_HEADS, SWT_WIN, SWT_Q), BF16),
            pltpu.VMEM((2, SWT_HEADS, 1, SWT_Q), F32),
        ],
        compiler_params=_cparams(2),
        name="swa_attn",
    )(sinks, proj, proj, proj, proj, cos_t, sin_t, by_item(cos_t), by_item(sin_t))


MAMBA_SUB = 4


def _mamba_kernel(xbc_ref, z_ref, sm_ref, cw_ref, cb_ref, dtb_ref, alog_ref, dsk_ref, nw_ref, o_ref,
                  prev_ref, state_ref):
    @pl.when(pl.program_id(1) == 0)
    def _():
        prev_ref[...] = jnp.zeros_like(prev_ref)
        state_ref[...] = jnp.zeros_like(state_ref)

    prev = prev_ref[...]
    for sub in range(MAMBA_SUB):
        rows = slice(sub * CHUNK, (sub + 1) * CHUNK)
        _mamba_chunk(xbc_ref, z_ref, sm_ref, cw_ref, cb_ref, dtb_ref, alog_ref, dsk_ref, nw_ref, o_ref, state_ref,
                     rows, prev)
        prev = xbc_ref[rows, :]
    prev_ref[...] = prev


def _mamba_chunk(xbc_ref, z_ref, sm_ref, cw_ref, cb_ref, dtb_ref, alog_ref, dsk_ref, nw_ref, o_ref, state_ref,
                 rows, prev):
    n = CHUNK
    cur = xbc_ref[rows, :]
    both = jnp.concatenate([prev, cur], axis=0)
    srow = lax.broadcasted_iota(jnp.int32, (n, 2 * n), 0)
    scol = lax.broadcasted_iota(jnp.int32, (n, 2 * n), 1)
    shifts = jnp.concatenate([(scol == srow + (n - (CONV_WIDTH - 1) + k)).astype(BF16)
                              for k in range(CONV_WIDTH - 1)], axis=0)
    shifted = _dot(shifts, both)
    conv = cb_ref[...] + cw_ref[CONV_WIDTH - 1:CONV_WIDTH, :] * cur.astype(F32)
    for k in range(CONV_WIDTH - 1):
        conv = conv + cw_ref[k:k + 1, :] * shifted[k * n:(k + 1) * n, :]
    xbc = _silu(conv)
    xs = xbc[:, :D_MODEL]
    bm = xbc[:, D_MODEL:D_MODEL + SSM_GROUPS * SSM_STATE]
    cm = xbc[:, D_MODEL + SSM_GROUPS * SSM_STATE:]
    xs_b = xs.astype(BF16)

    dt = _softplus(sm_ref[rows, :] + dtb_ref[...])
    a_dt = dt * (jnp.exp(alog_ref[...]) * (-LOG2E))
    row = lax.broadcasted_iota(jnp.int32, (n, n), 0)
    col = lax.broadcasted_iota(jnp.int32, (n, n), 1)
    causal = row >= col
    tri = causal.astype(BF16)
    loc = _dot(tri, jnp.concatenate(_split3(a_dt), axis=1))
    acum = loc[:, :LANES] + loc[:, LANES:2 * LANES] + loc[:, 2 * LANES:]
    acum_t = acum.T
    dt_t = dt.T
    eacum = jnp.exp2(acum)
    lane = lax.broadcasted_iota(jnp.int32, (n, LANES), 1)
    even_head = lane < HEAD_DIM

    ys = []
    for g in range(SSM_GROUPS):
        gcols = slice(g * SSM_STATE, (g + 1) * SSM_STATE)
        bg = bm[:, gcols]
        cg = cm[:, gcols]
        cbm = _dot_nt(cg.astype(BF16), bg.astype(BF16))
        bg_t = bg.T
        for pp in range(2):
            pair = 2 * g + pp
            pcols = slice(pair * LANES, (pair + 1) * LANES)
            st_old = state_ref[:, pcols]
            rhs = jnp.concatenate([xs_b[:, pcols], st_old.astype(BF16)], axis=0)
            outs, news, sdec = [], [], []
            for e in range(2):
                h = 2 * pair + e
                a_col = acum[:, h:h + 1]
                a_row = acum_t[h:h + 1, :]
                dt_row = dt_t[h:h + 1, :]
                decay = jnp.exp2(jnp.where(causal, a_col - a_row, NEG))
                m_in = (cbm * decay * dt_row).astype(BF16)
                c_off = (cg * eacum[:, h:h + 1]).astype(BF16)
                outs.append(_dot(jnp.concatenate([m_in, c_off], axis=1), rhs))
                a_last = a_row[:, n - 1:n]
                w_row = jnp.exp2(a_last - a_row) * dt_row
                news.append(_dot((bg_t * w_row).astype(BF16), xs_b[:, pcols]))
                sdec.append(jnp.exp2(a_last))
            ys.append(jnp.where(even_head, outs[0], outs[1]))
            state_ref[:, pcols] = (st_old * jnp.where(even_head, sdec[0], sdec[1])
                                   + jnp.where(even_head, news[0], news[1]))

    y = jnp.concatenate(ys, axis=1) + dsk_ref[...] * xs
    y = y * _silu(z_ref[rows, :].astype(F32))
    gw = D_MODEL // SSM_GROUPS
    for g in range(SSM_GROUPS):
        gc = slice(g * gw, (g + 1) * gw)
        yg = y[:, gc]
        ms = jnp.mean(yg * yg, axis=-1, keepdims=True)
        o_ref[rows, gc] = (yg * lax.rsqrt(ms + NORM_EPS) * nw_ref[:, gc]).astype(BF16)


def _mamba(proj, small, conv_w, conv_b, dtb_row, alog_row, dskip_full, norm_w, seq):
    t = proj.shape[0]
    b = t // seq
    tm = MAMBA_SUB * CHUNK
    nchunk = seq // tm
    xw = 2 * D_MODEL
    const = lambda shape: pl.BlockSpec(shape, lambda bi, c: (0,) * len(shape))
    return pl.pallas_call(
        _mamba_kernel,
        grid=(b, nchunk),
        in_specs=[
            pl.BlockSpec((tm, xw), lambda bi, c: (bi * nchunk + c, COL_A_XBC // xw)),
            pl.BlockSpec((tm, D_MODEL), lambda bi, c: (bi * nchunk + c, COL_A_Z // D_MODEL)),
            pl.BlockSpec((tm, LANES), lambda bi, c: (bi * nchunk + c, 0)),
            const((CONV_WIDTH, xw)), const((1, xw)), const((1, LANES)), const((1, LANES)),
            const((1, D_MODEL)), const((1, D_MODEL)),
        ],
        out_specs=pl.BlockSpec((tm, D_MODEL), lambda bi, c: (bi * nchunk + c, 0)),
        out_shape=jax.ShapeDtypeStruct((t, D_MODEL), BF16),
        scratch_shapes=[
            pltpu.VMEM((CHUNK, xw), BF16),
            pltpu.VMEM((SSM_STATE, D_MODEL), F32),
        ],
        compiler_params=_cparams(2),
        name="mamba",
    )(proj, proj, small, conv_w, conv_b, dtb_row, alog_row, dskip_full, norm_w)


OUT_TM = 512


def _merge_kernel(x_ref, ya_ref, yb_ref, yc_ref, g_ref, gb_ref, wp_ref, wo_ref, fnw_ref, o_ref, *, final_norm):
    merged = None
    for i, y_ref in enumerate((ya_ref, yb_ref, yc_ref)):
        branch = _dot(y_ref[...], wp_ref[i])
        gcols = slice(i * D_MODEL, (i + 1) * D_MODEL)
        gate = 1.0 / (1.0 + jnp.exp2((g_ref[:, gcols].astype(F32) + gb_ref[:, gcols]) * (-LOG2E)))
        merged = gate * branch if merged is None else merged + gate * branch
    x = x_ref[...] + _dot(merged.astype(BF16), wo_ref[...])
    if final_norm:
        ms = jnp.mean(x * x, axis=-1, keepdims=True)
        x = x * lax.rsqrt(ms + NORM_EPS) * fnw_ref[...]
    o_ref[...] = x


def _merge(x2, ya, yb, yc, proj, gate_bias, w_proj, w_out, final_norm_w, final_norm):
    t = x2.shape[0]
    gw = 3 * D_MODEL
    row = lambda w, cbi: pl.BlockSpec((OUT_TM, w), lambda i: (i, cbi))
    const = lambda shape: pl.BlockSpec(shape, lambda i: (0,) * len(shape))
    return pl.pallas_call(
        functools.partial(_merge_kernel, final_norm=final_norm),
        grid=(t // OUT_TM,),
        in_specs=[
            row(D_MODEL, 0), row(D_MODEL, 0), row(D_MODEL, 0), row(D_MODEL, 0),
            row(gw, COL_GATES // gw),
            const((1, gw)), const((3, D_MODEL, D_MODEL)), const((D_MODEL, D_MODEL)), const((1, D_MODEL)),
        ],
        out_specs=row(D_MODEL, 0),
        out_shape=jax.ShapeDtypeStruct((t, D_MODEL), F32),
        compiler_params=_cparams(1),
        name="merge_out",
    )(x2, ya, yb, yc, proj, gate_bias, w_proj, w_out, final_norm_w)


def _dup_heads(w):
    d = w.shape[0]
    w4 = w.reshape(d, SWA_KV_HEADS, HEAD_DIM)
    return jnp.concatenate([w4, w4], axis=-1).reshape(d, SWA_KV_HEADS * LANES)


def _layout_w_in(w):
    offs = np.cumsum((0,) + _IN_SIZES)
    (a_xbc, a_z, a_dt, b_q, b_k, b_v, b_z, c_q, c_k, c_v, c_f, c_z, gates) = [
        w[:, offs[i]:offs[i + 1]] for i in range(len(_IN_SIZES))]
    scale = HEAD_DIM ** -0.5
    big = jnp.concatenate([a_xbc, a_z, b_z, c_z, b_q * (scale * LOG2E), c_q * (scale * LOG2E), c_k, c_v, gates,
                           _dup_heads(b_k), _dup_heads(b_v)], axis=1).astype(BF16)
    small = jnp.concatenate([a_dt, c_f, jnp.zeros((w.shape[0], LANES - 32), w.dtype)], axis=1)
    s_hi = small.astype(BF16)
    s_lo = (small - s_hi.astype(F32)).astype(BF16)
    return big, jnp.stack([s_hi, s_lo])


def _lane_row(v, start):
    return jnp.zeros((1, LANES), F32).at[0, start:start + N_HEADS].set(v.astype(F32))


def _rope_tables(seq):
    pos = jnp.arange(seq, dtype=F32)
    inv_freq = ROPE_THETA ** (-jnp.arange(0, HEAD_DIM, 2, dtype=F32) / HEAD_DIM)
    ang = pos[:, None] * inv_freq[None, :]
    cos, sin = jnp.cos(ang), jnp.sin(ang)
    cos_t = jnp.concatenate([cos, cos, cos, cos], axis=1)
    sin_t = jnp.concatenate([-sin, sin, -sin, sin], axis=1)
    return cos_t, sin_t


def kernel(x, norm_w, w_in, conv_w, conv_b, dt_bias, a_log, d_skip, ssm_norm_w,
           sinks, f_bias, gate_bias, w_proj, w_out, final_norm_w):
    b, s, d = x.shape
    depth = norm_w.shape[0]
    assert d == D_MODEL and s % FOX_BLK == 0 and s >= 2 * FOX_BLK and (b * s) % IN_TM == 0
    cos_t, sin_t = _rope_tables(s)
    x2 = x.reshape(b * s, d)
    for layer in range(depth):
        w_big, w_small = _layout_w_in(w_in[layer])
        proj, small = _inproj(x2, norm_w[layer][None, :], w_big, w_small)
        kfeat, qfeat_t = _fox_scan(small, _lane_row(f_bias[layer], LANE_F), s)
        ya = _mamba(proj, small, conv_w[layer], conv_b[layer][None, :], _lane_row(dt_bias[layer], 0),
                    _lane_row(a_log[layer], 0), jnp.repeat(d_skip[layer], HEAD_DIM)[None, :],
                    ssm_norm_w[layer][None, :], s)
        yb = _swt_attention(proj, sinks[layer], cos_t, sin_t, s)
        yc = _fox_attention(proj, kfeat, qfeat_t, s)
        x2 = _merge(x2, ya, yb, yc, proj, gate_bias[layer].reshape(1, 3 * d),
                    w_proj[layer].astype(BF16), w_out[layer].astype(BF16), final_norm_w[None, :],
                    final_norm=(layer == depth - 1))
    return x2.reshape(b, s, d)
```

```python
import functools
import math

import jax
import jax.numpy as jnp
import numpy as np
from jax import lax
from jax.experimental import pallas as pl
from jax.experimental.pallas import tpu as pltpu

F32 = jnp.float32
BF16 = jnp.bfloat16

D_MODEL = 1024
HEAD_DIM = 64
N_HEADS = 16
N_PAIRS = N_HEADS // 2
LANES = 128
SSM_GROUPS = 4
SSM_STATE = 128
CONV_WIDTH = 4
CHUNK = 128
SWA_KV_HEADS = 4
ROPE_THETA = 10000.0
NORM_EPS = 1e-6
NEG = -1e30
LOG2E = math.log2(math.e)

COL_A_XBC = 0
COL_A_Z = 2048
COL_B_Z = 3072
COL_C_Z = 4096
COL_B_Q = 5120
COL_C_Q = 6144
COL_C_K = 7168
COL_C_V = 8192
COL_GATES = 9216
COL_B_K = 12288
COL_B_V = 12800
N_PROJ = 13312

_IN_SIZES = (2048, 1024, 16, 1024, 256, 256, 1024, 1024, 1024, 1024, 16, 1024, 3072)

VMEM_LIMIT = 56 * 1024 * 1024


def _cparams(n_axes):
    return pltpu.CompilerParams(dimension_semantics=("arbitrary",) * n_axes, vmem_limit_bytes=VMEM_LIMIT)


def _split3(x):
    hi = x.astype(BF16)
    r1 = x - hi.astype(F32)
    mid = r1.astype(BF16)
    lo = (r1 - mid.astype(F32)).astype(BF16)
    return hi, mid, lo


def _dot(a, b):
    return jnp.dot(a, b, preferred_element_type=F32)


def _dot_nt(a, b):
    return lax.dot_general(a, b, (((1,), (1,)), ((), ())), preferred_element_type=F32)


def _silu(x):
    return x * (1.0 / (1.0 + jnp.exp2(x * (-LOG2E))))


def _softplus(x):
    return jnp.maximum(x, 0.0) + jnp.log1p(jnp.exp(-jnp.abs(x)))


IN_TM = 2048
IN_TN = 1024
IN_NORM_ROWS = 512


def _inproj_kernel(x_ref, nw_ref, w_ref, wsm_ref, o_ref, osm_ref, h_ref):
    @pl.when(pl.program_id(1) == 0)
    def _():
        for r in range(IN_TM // IN_NORM_ROWS):
            rows = slice(r * IN_NORM_ROWS, (r + 1) * IN_NORM_ROWS)
            x = x_ref[rows, :]
            ms = jnp.mean(x * x, axis=-1, keepdims=True)
            h = x * lax.rsqrt(ms + NORM_EPS) * nw_ref[...]
            hh = h.astype(BF16)
            h_ref[rows, :] = hh
            hl = (h - hh.astype(F32)).astype(BF16)
            osm_ref[rows, :] = _dot(hh, wsm_ref[0]) + _dot(hl, wsm_ref[0]) + _dot(hh, wsm_ref[1])

    o_ref[...] = _dot(h_ref[...], w_ref[...]).astype(BF16)


def _inproj(x2, norm_w, w_big, w_small):
    t = x2.shape[0]
    grid = (t // IN_TM, N_PROJ // IN_TN)
    return pl.pallas_call(
        _inproj_kernel,
        grid=grid,
        in_specs=[
            pl.BlockSpec((IN_TM, D_MODEL), lambda i, j: (i, 0)),
            pl.BlockSpec((1, D_MODEL), lambda i, j: (0, 0)),
            pl.BlockSpec((D_MODEL, IN_TN), lambda i, j: (0, j)),
            pl.BlockSpec((2, D_MODEL, LANES), lambda i, j: (0, 0, 0)),
        ],
        out_specs=[
            pl.BlockSpec((IN_TM, IN_TN), lambda i, j: (i, j)),
            pl.BlockSpec((IN_TM, LANES), lambda i, j: (i, 0)),
        ],
        out_shape=[
            jax.ShapeDtypeStruct((t, N_PROJ), BF16),
            jax.ShapeDtypeStruct((t, LANES), F32),
        ],
        scratch_shapes=[pltpu.VMEM((IN_TM, D_MODEL), BF16)],
        compiler_params=_cparams(2),
        name="inproj",
    )(x2, norm_w, w_big, w_small)


LANE_F = 16


def _scan_selectors():
    selk = np.zeros((3 * LANES, N_PAIRS * LANES), np.float32)
    konst = np.zeros((1, N_PAIRS * LANES), np.float32)
    selq = np.zeros((LANES, 3 * LANES), np.float32)
    qonst = np.zeros((LANES, 1), np.float32)
    for p in range(N_PAIRS):
        for e in range(2):
            src = LANE_F + 2 * p + e
            kbase = p * LANES + (64 if e == 0 else 0)
            qbase = 16 * p + 8 * e
            for t in range(3):
                konst[0, kbase + t] = 1.0
                selk[t * LANES + src, kbase + 3 + t] = 1.0
                selq[qbase + t, t * LANES + src] = 1.0
                qonst[qbase + 3 + t, 0] = -1.0
    return selk, konst, selq, qonst


SCAN_UNROLL = 4


def _exact_prefix(mask_bf16, x):
    y = _dot(mask_bf16, jnp.concatenate(_split3(x), axis=1))
    return y[:, :LANES] + y[:, LANES:2 * LANES] + y[:, 2 * LANES:]


def _scan_kernel(sm_ref, fb_ref, selk_ref, konst_ref, selq_ref, qonst_ref, kf_ref, qf_ref, loc_ref, tot_ref):
    nchunk = sm_ref.shape[0] // CHUNK
    assert nchunk % SCAN_UNROLL == 0
    row = lax.broadcasted_iota(jnp.int32, (CHUNK, CHUNK), 0)
    col = lax.broadcasted_iota(jnp.int32, (CHUNK, CHUNK), 1)
    tri = (row >= col).astype(BF16)
    fb = fb_ref[...]
    chunk_rows = lambda c: pl.ds(pl.multiple_of(c * CHUNK, CHUNK), CHUNK)

    def unrolled(body):
        def trip(g, carry):
            for i in range(SCAN_UNROLL):
                body(g * SCAN_UNROLL + i)
            return carry
        lax.fori_loop(0, nchunk // SCAN_UNROLL, trip, 0)

    def local_scan(c):
        v = sm_ref[chunk_rows(c), :] + fb
        lf = (jnp.minimum(v, 0.0) - jnp.log1p(jnp.exp(-jnp.abs(v)))) * LOG2E
        loc = _exact_prefix(tri, lf)
        loc_ref[chunk_rows(c), :] = loc
        tot_ref[pl.ds(c, 1), :] = loc[CHUNK - 1:CHUNK, :]

    unrolled(local_scan)

    crow = lax.broadcasted_iota(jnp.int32, (nchunk, nchunk), 0)
    ccol = lax.broadcasted_iota(jnp.int32, (nchunk, nchunk), 1)
    tot_ref[...] = _exact_prefix((crow > ccol).astype(BF16), tot_ref[...])

    def emit(c):
        cum = loc_ref[chunk_rows(c), :] + tot_ref[pl.ds(c, 1), :]
        c3 = jnp.concatenate(_split3(cum), axis=1)
        kf_ref[chunk_rows(c), :] = (_dot(c3, selk_ref[...]) + konst_ref[...]).astype(BF16)
        qf_ref[c] = _dot_nt(selq_ref[...], c3) + qonst_ref[...]

    unrolled(emit)


def _fox_scan(small, fb_row, seq):
    t = small.shape[0]
    b = t // seq
    selk, konst, selq, qonst = _scan_selectors()
    const = lambda shape: pl.BlockSpec(shape, lambda i: (0,) * len(shape))
    return pl.pallas_call(
        _scan_kernel,
        grid=(b,),
        in_specs=[
            pl.BlockSpec((seq, LANES), lambda i: (i, 0)),
            const((1, LANES)),
            const(selk.shape), const(konst.shape), const(selq.shape), const(qonst.shape),
        ],
        out_specs=[
            pl.BlockSpec((seq, N_PAIRS * LANES), lambda i: (i, 0)),
            pl.BlockSpec((None, seq // CHUNK, LANES, CHUNK), lambda i: (i, 0, 0, 0)),
        ],
        out_shape=[
            jax.ShapeDtypeStruct((t, N_PAIRS * LANES), BF16),
            jax.ShapeDtypeStruct((b, seq // CHUNK, LANES, CHUNK), F32),
        ],
        scratch_shapes=[
            pltpu.VMEM((seq, LANES), F32),
            pltpu.VMEM((seq // CHUNK, LANES), F32),
        ],
        compiler_params=_cparams(1),
        name="fox_scan",
    )(small, fb_row, jnp.asarray(selk, BF16), jnp.asarray(konst, F32), jnp.asarray(selq, BF16),
      jnp.asarray(qonst, F32))


FOX_BLK = 256
FOX_PAIRS = 2
FOX_HEADS = 2 * FOX_PAIRS
FOX_UNROLL = 4
FOX_VROWS = HEAD_DIM + 16


def _fox_schedule(n_blocks):
    items = [(q, q) for q in range(n_blocks)] + [(q, j) for q in range(n_blocks) for j in range(q)]
    return np.asarray([it[0] for it in items], np.int32), np.asarray([it[1] for it in items], np.int32)


def _fox_kernel(tq_ref, tj_ref, q_ref, k_ref, v_ref, kf_ref, qf_ref, z_ref, o_ref,
                    kaug_ref, vt_ref, qt_ref, acc_ref, m_ref, a_ref, mx_ref, st_ref, p_ref):
    seq = k_ref.shape[0]
    nb = FOX_BLK
    n_blocks = seq // nb
    n_items = n_blocks * (n_blocks + 1) // 2
    blk_rows = lambda i: pl.ds(pl.multiple_of(i * nb, nb), nb)

    lane = lax.broadcasted_iota(jnp.int32, (nb, LANES), 1)
    ones = jnp.ones((FOX_VROWS - HEAD_DIM, nb), BF16)
    zeros56 = jnp.zeros((HEAD_DIM - 8, nb), F32)

    def setup(c, carry):
        rows = blk_rows(c)
        for pr in range(FOX_PAIRS):
            cols = slice(pr * LANES, (pr + 1) * LANES)
            k2 = k_ref[rows, cols]
            kf = kf_ref[rows, cols]
            kaug_ref[2 * pr, rows, :] = jnp.where(lane < HEAD_DIM, k2, kf)
            kaug_ref[2 * pr + 1, rows, :] = jnp.where(lane >= HEAD_DIM, k2, kf)
            vt = v_ref[rows, cols].astype(F32).T.astype(BF16)
            vt_ref[2 * pr, c] = jnp.concatenate([vt[:HEAD_DIM], ones], axis=0)
            vt_ref[2 * pr + 1, c] = jnp.concatenate([vt[HEAD_DIM:], ones], axis=0)
            qt = q_ref[rows, cols].astype(F32).T
            qf = jnp.concatenate([qf_ref[(nb // CHUNK) * c + s, 16 * pr:16 * (pr + 1), :]
                                  for s in range(nb // CHUNK)], axis=1)
            qt_ref[c, 2 * pr] = jnp.concatenate([qt[:HEAD_DIM], qf[:8], zeros56], axis=0).astype(BF16)
            qt_ref[c, 2 * pr + 1] = jnp.concatenate([qf[8:], zeros56, qt[HEAD_DIM:]], axis=0).astype(BF16)
        return carry

    def pairwise(body):
        assert n_blocks % 2 == 0
        lax.fori_loop(0, n_blocks // 2, lambda g, carry: body(2 * g + 1, body(2 * g, carry)), 0)

    pairwise(setup)
    acc_ref[...] = jnp.zeros_like(acc_ref)
    m_ref[...] = jnp.full(m_ref.shape, NEG, F32)

    def scores(u, slot):
        qb, kb = tq_ref[u], tj_ref[u]
        for h in range(FOX_HEADS):
            st = _dot(kaug_ref[h, blk_rows(kb), :], qt_ref[qb, h])
            st_ref[slot, h] = st
            mx_ref[slot, h] = jnp.max(st, axis=0, keepdims=True)

    def softmax(u, slot, masked):
        qb = tq_ref[u]
        if masked:
            keep = (lax.broadcasted_iota(jnp.int32, (nb, nb), 0) <= lax.broadcasted_iota(jnp.int32, (nb, nb), 1))
        for h in range(FOX_HEADS):
            st = st_ref[slot, h]
            if masked:
                st = jnp.where(keep, st, NEG)
                blk_max = jnp.max(st, axis=0, keepdims=True)
            else:
                blk_max = mx_ref[slot, h]
            m_old = m_ref[qb, h]
            m_new = jnp.maximum(m_old, blk_max)
            p_ref[slot, h] = jnp.exp2(st - m_new).astype(BF16)
            a_ref[slot, h] = jnp.exp2(m_old - m_new)
            m_ref[qb, h] = m_new

    def values(u, slot):
        qb, kb = tq_ref[u], tj_ref[u]
        for h in range(FOX_HEADS):
            acc_ref[qb, h] = a_ref[slot, h] * acc_ref[qb, h] + _dot(vt_ref[h, kb], p_ref[slot, h])

    def step(u, slot, masked):
        values(u - 2, slot)
        softmax(u - 1, 1 - slot, masked)
        scores(u, slot)

    def run(lo, hi, masked):
        while lo < hi and lo % FOX_UNROLL:
            step(lo, lo % 2, masked)
            lo += 1
        trips = (hi - lo) // FOX_UNROLL
        if trips:
            def trip(g, carry):
                for i in range(FOX_UNROLL):
                    step(lo + g * FOX_UNROLL + i, i % 2, masked)
                return carry
            lax.fori_loop(0, trips, trip, 0)
        for u in range(lo + trips * FOX_UNROLL, hi):
            step(u, u % 2, masked)

    scores(0, 0)
    scores(1, 1)
    softmax(0, 0, True)
    run(2, n_blocks + 1, True)
    run(n_blocks + 1, n_items, False)
    values(n_items - 2, n_items % 2)
    softmax(n_items - 1, (n_items - 1) % 2, False)
    values(n_items - 1, (n_items - 1) % 2)

    def finish(c, carry):
        rows = blk_rows(c)
        for pr in range(FOX_PAIRS):
            ot = jnp.concatenate([acc_ref[c, 2 * pr + e, :HEAD_DIM, :]
                                  * (1.0 / acc_ref[c, 2 * pr + e, HEAD_DIM:HEAD_DIM + 1, :]) for e in range(2)], axis=0)
            cols = slice(pr * LANES, (pr + 1) * LANES)
            o_ref[rows, cols] = (ot.T * _silu(z_ref[rows, cols].astype(F32))).astype(BF16)
        return carry

    pairwise(finish)


def _fox_attention(proj, kfeat, qfeat_t, seq):
    t = proj.shape[0]
    b = t // seq
    nblk = seq // FOX_BLK
    w = FOX_PAIRS * LANES
    cb = lambda off: off // w
    tq, tj = _fox_schedule(nblk)
    col = lambda off: pl.BlockSpec((seq, w), lambda bi, p: (bi, cb(off) + p))
    return pl.pallas_call(
        _fox_kernel,
        grid=(b, N_PAIRS // FOX_PAIRS),
        in_specs=[
            pl.BlockSpec(memory_space=pltpu.SMEM), pl.BlockSpec(memory_space=pltpu.SMEM),
            col(COL_C_Q), col(COL_C_K), col(COL_C_V),
            pl.BlockSpec((seq, w), lambda bi, p: (bi, p)),
            pl.BlockSpec((None, seq // CHUNK, 16 * FOX_PAIRS, CHUNK), lambda bi, p: (bi, 0, p, 0)),
            col(COL_C_Z),
        ],
        out_specs=pl.BlockSpec((seq, w), lambda bi, p: (bi, p)),
        out_shape=jax.ShapeDtypeStruct((t, D_MODEL), BF16),
        scratch_shapes=[
            pltpu.VMEM((FOX_HEADS, seq, LANES), BF16),
            pltpu.VMEM((FOX_HEADS, nblk, FOX_VROWS, FOX_BLK), BF16),
            pltpu.VMEM((nblk, FOX_HEADS, LANES, FOX_BLK), BF16),
            pltpu.VMEM((nblk, FOX_HEADS, FOX_VROWS, FOX_BLK), F32),
            pltpu.VMEM((nblk, FOX_HEADS, 1, FOX_BLK), F32),
            pltpu.VMEM((2, FOX_HEADS, 1, FOX_BLK), F32),
            pltpu.VMEM((2, FOX_HEADS, 1, FOX_BLK), F32),
            pltpu.VMEM((2, FOX_HEADS, FOX_BLK, FOX_BLK), F32),
            pltpu.VMEM((2, FOX_HEADS, FOX_BLK, FOX_BLK), BF16),
        ],
        compiler_params=_cparams(2),
        name="fox_attn",
    )(jnp.asarray(tq), jnp.asarray(tj), proj, proj, proj, kfeat, qfeat_t, proj)


SWT_Q = 256
SWT_WIN = SWT_Q + CHUNK
SWT_HEADS = 4
SWT_VROWS = HEAD_DIM + 16
SWT_UNROLL = 4


def _rope(x, cos, sin_signed, first_half):
    partner = jnp.where(first_half, pltpu.roll(x, 96, 1), pltpu.roll(x, 32, 1))
    return x * cos + partner * sin_signed


def _swt_kernel(sinks_ref, q_ref, k_ref, v_ref, z_ref, cos_ref, sin_ref, cost_ref, sint_ref, o_ref,
                kaug_ref, vt_ref, qt_ref, st_ref, p_ref, e_ref):
    g = pl.program_id(1)
    seq = k_ref.shape[0]
    nq = SWT_Q
    n_items = seq // nq
    lane = lax.broadcasted_iota(jnp.int32, (nq, LANES), 1)
    first_half = (lane % HEAD_DIM) < (HEAD_DIM // 2)
    even_head = lane < HEAD_DIM
    q_rows = lambda c: pl.ds(pl.multiple_of(c * nq, nq), nq)
    zero_k = jnp.zeros((nq, LANES), BF16)
    ones = jnp.ones((SWT_VROWS - HEAD_DIM, CHUNK), BF16)

    kaug_ref[:, 0:CHUNK, :] = jnp.zeros((2, CHUNK, LANES), BF16)
    vt_ref[0] = jnp.zeros((SWT_VROWS, CHUNK), BF16)

    def setup(c, carry):
        rows = q_rows(c)
        cos, sin = cos_ref[rows, :], sin_ref[rows, :]
        kr = _rope(k_ref[rows, :].astype(F32), cos, sin, first_half).astype(BF16)
        dst = pl.ds(pl.multiple_of(c * nq + CHUNK, CHUNK), nq)
        kaug_ref[0, dst, :] = jnp.where(even_head, kr, zero_k)
        kaug_ref[1, dst, :] = jnp.where(even_head, zero_k, kr)
        vt = v_ref[rows, :].astype(F32).T.astype(BF16)
        for s in range(nq // CHUNK):
            vt_ref[(nq // CHUNK) * c + 1 + s] = jnp.concatenate(
                [vt[:HEAD_DIM, s * CHUNK:(s + 1) * CHUNK], ones], axis=0)
        half = HEAD_DIM // 2
        for pr in range(SWT_HEADS // 2):
            cols = slice(pr * LANES, (pr + 1) * LANES)
            qt = q_ref[rows, cols].astype(F32).T
            partner = jnp.concatenate([qt[half:2 * half], qt[:half], qt[3 * half:], qt[2 * half:3 * half]], axis=0)
            qt_ref[c, pr] = (qt * cost_ref[c] + partner * sint_ref[c]).astype(BF16)
        return carry

    lax.fori_loop(0, n_items // 2, lambda i, carry: setup(2 * i + 1, setup(2 * i, carry)), 0)

    kidx = lax.broadcasted_iota(jnp.int32, (SWT_WIN, nq), 0)
    qidx = lax.broadcasted_iota(jnp.int32, (SWT_WIN, nq), 1)

    def scores(c, slot):
        win = pl.ds(pl.multiple_of(c * nq, nq), SWT_WIN)
        for h in range(SWT_HEADS):
            st_ref[slot, h] = _dot(kaug_ref[h % 2, win, :], qt_ref[c, h // 2])

    def softmax(slot, first_item):
        band = (kidx > qidx) & (kidx <= qidx + CHUNK)
        keep = band & (kidx >= CHUNK) if first_item else band
        for h in range(SWT_HEADS):
            sink = sinks_ref[SWT_HEADS * g + h] * LOG2E
            st = jnp.where(keep, st_ref[slot, h], NEG)
            m = jnp.maximum(jnp.max(st, axis=0, keepdims=True), sink)
            p_ref[slot, h] = jnp.exp2(st - m).astype(BF16)
            e_ref[slot, h] = jnp.exp2(sink - m)

    def values(c, slot):
        vth = jnp.concatenate([vt_ref[(nq // CHUNK) * c + s] for s in range(SWT_WIN // CHUNK)], axis=1)
        rows = q_rows(c)
        for pr in range(SWT_HEADS // 2):
            halves = []
            for e in range(2):
                h = 2 * pr + e
                o = _dot(vth, p_ref[slot, h])
                halves.append(o[:HEAD_DIM] * (1.0 / (o[HEAD_DIM:HEAD_DIM + 1] + e_ref[slot, h])))
            cols = slice(pr * LANES, (pr + 1) * LANES)
            y = jnp.concatenate(halves, axis=0).T
            o_ref[rows, cols] = (y * _silu(z_ref[rows, cols].astype(F32))).astype(BF16)

    def step(u, slot):
        values(u - 2, slot)
        softmax(1 - slot, False)
        scores(u, slot)

    scores(0, 0)
    scores(1, 1)
    softmax(0, True)
    assert n_items % 2 == 0 and n_items >= 4

    n_trips = (n_items - 2) // SWT_UNROLL

    def trip(i, carry):
        for j in range(SWT_UNROLL):
            step(2 + SWT_UNROLL * i + j, j % 2)
        return carry

    lax.fori_loop(0, n_trips, trip, 0)
    for u in range(2 + SWT_UNROLL * n_trips, n_items):
        step(u, u % 2)
    values(n_items - 2, 0)
    softmax(1, False)
    values(n_items - 1, 1)


def _swt_attention(proj, sinks, cos_t, sin_t, seq):
    t = proj.shape[0]
    b = t // seq
    w = (SWT_HEADS // 2) * LANES
    n_items = seq // SWT_Q
    by_item = lambda tbl: tbl.T.reshape(LANES, n_items, SWT_Q).transpose(1, 0, 2)
    return pl.pallas_call(
        _swt_kernel,
        grid=(b, SWA_KV_HEADS),
        in_specs=[
            pl.BlockSpec(memory_space=pltpu.SMEM),
            pl.BlockSpec((seq, w), lambda bi, g: (bi, COL_B_Q // w + g)),
            pl.BlockSpec((seq, LANES), lambda bi, g: (bi, COL_B_K // LANES + g)),
            pl.BlockSpec((seq, LANES), lambda bi, g: (bi, COL_B_V // LANES + g)),
            pl.BlockSpec((seq, w), lambda bi, g: (bi, COL_B_Z // w + g)),
            pl.BlockSpec((seq, LANES), lambda bi, g: (0, 0)),
            pl.BlockSpec((seq, LANES), lambda bi, g: (0, 0)),
            pl.BlockSpec((n_items, LANES, SWT_Q), lambda bi, g: (0, 0, 0)),
            pl.BlockSpec((n_items, LANES, SWT_Q), lambda bi, g: (0, 0, 0)),
        ],
        out_specs=pl.BlockSpec((seq, w), lambda bi, g: (bi, g)),
        out_shape=jax.ShapeDtypeStruct((t, D_MODEL), BF16),
        scratch_shapes=[
            pltpu.VMEM((2, seq + CHUNK, LANES), BF16),
            pltpu.VMEM((seq // CHUNK + 1, SWT_VROWS, CHUNK), BF16),
            pltpu.VMEM((n_items, SWT_HEADS // 2, LANES, SWT_Q), BF16),
            pltpu.VMEM((2, SWT_HEADS, SWT_WIN, SWT_Q), F32),
            pltpu.VMEM((2, SWT_HEADS, SWT_WIN, SWT_Q), BF16),
            pltpu.VMEM((2, SWT_HEADS, 1, SWT_Q), F32),
        ],
        compiler_params=_cparams(2),
        name="swa_attn",
    )(sinks, proj, proj, proj, proj, cos_t, sin_t, by_item(cos_t), by_item(sin_t))


MAMBA_SUB = 8


def _mamba_kernel(xbc_ref, z_ref, sm_ref, cw_ref, cb_ref, dtb_ref, alog_ref, dsk_ref, nw_ref, o_ref,
                  prev_ref, state_ref):
    @pl.when(pl.program_id(1) == 0)
    def _():
        prev_ref[...] = jnp.zeros_like(prev_ref)
        state_ref[...] = jnp.zeros_like(state_ref)

    prev = prev_ref[...]
    for sub in range(MAMBA_SUB):
        rows = slice(sub * CHUNK, (sub + 1) * CHUNK)
        _mamba_chunk(xbc_ref, z_ref, sm_ref, cw_ref, cb_ref, dtb_ref, alog_ref, dsk_ref, nw_ref, o_ref, state_ref,
                     rows, prev)
        prev = xbc_ref[rows, :]
    prev_ref[...] = prev


def _mamba_chunk(xbc_ref, z_ref, sm_ref, cw_ref, cb_ref, dtb_ref, alog_ref, dsk_ref, nw_ref, o_ref, state_ref,
                 rows, prev):
    n = CHUNK
    cur = xbc_ref[rows, :]
    both = jnp.concatenate([prev, cur], axis=0)
    srow = lax.broadcasted_iota(jnp.int32, (n, 2 * n), 0)
    scol = lax.broadcasted_iota(jnp.int32, (n, 2 * n), 1)
    shifts = jnp.concatenate([(scol == srow + (n - (CONV_WIDTH - 1) + k)).astype(BF16)
                              for k in range(CONV_WIDTH - 1)], axis=0)
    shifted = _dot(shifts, both)
    conv = cb_ref[...] + cw_ref[CONV_WIDTH - 1:CONV_WIDTH, :] * cur.astype(F32)
    for k in range(CONV_WIDTH - 1):
        conv = conv + cw_ref[k:k + 1, :] * shifted[k * n:(k + 1) * n, :]
    xbc = _silu(conv)
    xs = xbc[:, :D_MODEL]
    bm = xbc[:, D_MODEL:D_MODEL + SSM_GROUPS * SSM_STATE]
    cm = xbc[:, D_MODEL + SSM_GROUPS * SSM_STATE:]
    xs_b = xs.astype(BF16)

    dt = _softplus(sm_ref[rows, :] + dtb_ref[...])
    a_dt = dt * (jnp.exp(alog_ref[...]) * (-LOG2E))
    row = lax.broadcasted_iota(jnp.int32, (n, n), 0)
    col = lax.broadcasted_iota(jnp.int32, (n, n), 1)
    causal = row >= col
    tri = causal.astype(BF16)
    loc = _dot(tri, jnp.concatenate(_split3(a_dt), axis=1))
    acum = loc[:, :LANES] + loc[:, LANES:2 * LANES] + loc[:, 2 * LANES:]
    acum_t = acum.T
    dt_t = dt.T
    eacum = jnp.exp2(acum)
    lane = lax.broadcasted_iota(jnp.int32, (n, LANES), 1)
    even_head = lane < HEAD_DIM

    ys = []
    for g in range(SSM_GROUPS):
        gcols = slice(g * SSM_STATE, (g + 1) * SSM_STATE)
        bg = bm[:, gcols]
        cg = cm[:, gcols]
        cbm = _dot_nt(cg.astype(BF16), bg.astype(BF16))
        bg_t = bg.T
        for pp in range(2):
            pair = 2 * g + pp
            pcols = slice(pair * LANES, (pair + 1) * LANES)
            st_old = state_ref[:, pcols]
            rhs = jnp.concatenate([xs_b[:, pcols], st_old.astype(BF16)], axis=0)
            outs, news, sdec = [], [], []
            for e in range(2):
                h = 2 * pair + e
                a_col = acum[:, h:h + 1]
                a_row = acum_t[h:h + 1, :]
                dt_row = dt_t[h:h + 1, :]
                decay = jnp.exp2(jnp.where(causal, a_col - a_row, NEG))
                m_in = (cbm * decay * dt_row).astype(BF16)
                c_off = (cg * eacum[:, h:h + 1]).astype(BF16)
                outs.append(_dot(jnp.concatenate([m_in, c_off], axis=1), rhs))
                a_last = a_row[:, n - 1:n]
                w_row = jnp.exp2(a_last - a_row) * dt_row
                news.append(_dot((bg_t * w_row).astype(BF16), xs_b[:, pcols]))
                sdec.append(jnp.exp2(a_last))
            ys.append(jnp.where(even_head, outs[0], outs[1]))
            state_ref[:, pcols] = (st_old * jnp.where(even_head, sdec[0], sdec[1])
                                   + jnp.where(even_head, news[0], news[1]))

    y = jnp.concatenate(ys, axis=1) + dsk_ref[...] * xs
    y = y * _silu(z_ref[rows, :].astype(F32))
    gw = D_MODEL // SSM_GROUPS
    for g in range(SSM_GROUPS):
        gc = slice(g * gw, (g + 1) * gw)
        yg = y[:, gc]
        ms = jnp.mean(yg * yg, axis=-1, keepdims=True)
        o_ref[rows, gc] = (yg * lax.rsqrt(ms + NORM_EPS) * nw_ref[:, gc]).astype(BF16)


def _mamba(proj, small, conv_w, conv_b, dtb_row, alog_row, dskip_full, norm_w, seq):
    t = proj.shape[0]
    b = t // seq
    tm = MAMBA_SUB * CHUNK
    nchunk = seq // tm
    xw = 2 * D_MODEL
    const = lambda shape: pl.BlockSpec(shape, lambda bi, c: (0,) * len(shape))
    return pl.pallas_call(
        _mamba_kernel,
        grid=(b, nchunk),
        in_specs=[
            pl.BlockSpec((tm, xw), lambda bi, c: (bi * nchunk + c, COL_A_XBC // xw)),
            pl.BlockSpec((tm, D_MODEL), lambda bi, c: (bi * nchunk + c, COL_A_Z // D_MODEL)),
            pl.BlockSpec((tm, LANES), lambda bi, c: (bi * nchunk + c, 0)),
            const((CONV_WIDTH, xw)), const((1, xw)), const((1, LANES)), const((1, LANES)),
            const((1, D_MODEL)), const((1, D_MODEL)),
        ],
        out_specs=pl.BlockSpec((tm, D_MODEL), lambda bi, c: (bi * nchunk + c, 0)),
        out_shape=jax.ShapeDtypeStruct((t, D_MODEL), BF16),
        scratch_shapes=[
            pltpu.VMEM((CHUNK, xw), BF16),
            pltpu.VMEM((SSM_STATE, D_MODEL), F32),
        ],
        compiler_params=_cparams(2),
        name="mamba",
    )(proj, proj, small, conv_w, conv_b, dtb_row, alog_row, dskip_full, norm_w)


OUT_TM = 512


def _merge_kernel(x_ref, ya_ref, yb_ref, yc_ref, g_ref, gb_ref, wp_ref, wo_ref, fnw_ref, o_ref, *, final_norm):
    merged = None
    for i, y_ref in enumerate((ya_ref, yb_ref, yc_ref)):
        branch = _dot(y_ref[...], wp_ref[i])
        gcols = slice(i * D_MODEL, (i + 1) * D_MODEL)
        gate = 1.0 / (1.0 + jnp.exp2((g_ref[:, gcols].astype(F32) + gb_ref[:, gcols]) * (-LOG2E)))
        merged = gate * branch if merged is None else merged + gate * branch
    x = x_ref[...] + _dot(merged.astype(BF16), wo_ref[...])
    if final_norm:
        ms = jnp.mean(x * x, axis=-1, keepdims=True)
        x = x * lax.rsqrt(ms + NORM_EPS) * fnw_ref[...]
    o_ref[...] = x


def _merge(x2, ya, yb, yc, proj, gate_bias, w_proj, w_out, final_norm_w, final_norm):
    t = x2.shape[0]
    gw = 3 * D_MODEL
    row = lambda w, cbi: pl.BlockSpec((OUT_TM, w), lambda i: (i, cbi))
    const = lambda shape: pl.BlockSpec(shape, lambda i: (0,) * len(shape))
    return pl.pallas_call(
        functools.partial(_merge_kernel, final_norm=final_norm),
        grid=(t // OUT_TM,),
        in_specs=[
            row(D_MODEL, 0), row(D_MODEL, 0), row(D_MODEL, 0), row(D_MODEL, 0),
            row(gw, COL_GATES // gw),
            const((1, gw)), const((3, D_MODEL, D_MODEL)), const((D_MODEL, D_MODEL)), const((1, D_MODEL)),
        ],
        out_specs=row(D_MODEL, 0),
        out_shape=jax.ShapeDtypeStruct((t, D_MODEL), F32),
        compiler_params=_cparams(1),
        name="merge_out",
    )(x2, ya, yb, yc, proj, gate_bias, w_proj, w_out, final_norm_w)


def _dup_heads(w):
    d = w.shape[0]
    w4 = w.reshape(d, SWA_KV_HEADS, HEAD_DIM)
    return jnp.concatenate([w4, w4], axis=-1).reshape(d, SWA_KV_HEADS * LANES)


def _layout_w_in(w):
    offs = np.cumsum((0,) + _IN_SIZES)
    (a_xbc, a_z, a_dt, b_q, b_k, b_v, b_z, c_q, c_k, c_v, c_f, c_z, gates) = [
        w[:, offs[i]:offs[i + 1]] for i in range(len(_IN_SIZES))]
    scale = HEAD_DIM ** -0.5
    big = jnp.concatenate([a_xbc, a_z, b_z, c_z, b_q * (scale * LOG2E), c_q * (scale * LOG2E), c_k, c_v, gates,
                           _dup_heads(b_k), _dup_heads(b_v)], axis=1).astype(BF16)
    small = jnp.concatenate([a_dt, c_f, jnp.zeros((w.shape[0], LANES - 32), w.dtype)], axis=1)
    s_hi = small.astype(BF16)
    s_lo = (small - s_hi.astype(F32)).astype(BF16)
    return big, jnp.stack([s_hi, s_lo])


def _lane_row(v, start):
    return jnp.zeros((1, LANES), F32).at[0, start:start + N_HEADS].set(v.astype(F32))


def _rope_tables(seq):
    pos = jnp.arange(seq, dtype=F32)
    inv_freq = ROPE_THETA ** (-jnp.arange(0, HEAD_DIM, 2, dtype=F32) / HEAD_DIM)
    ang = pos[:, None] * inv_freq[None, :]
    cos, sin = jnp.cos(ang), jnp.sin(ang)
    cos_t = jnp.concatenate([cos, cos, cos, cos], axis=1)
    sin_t = jnp.concatenate([-sin, sin, -sin, sin], axis=1)
    return cos_t, sin_t


def kernel(x, norm_w, w_in, conv_w, conv_b, dt_bias, a_log, d_skip, ssm_norm_w,
           sinks, f_bias, gate_bias, w_proj, w_out, final_norm_w):
    b, s, d = x.shape
    depth = norm_w.shape[0]
    assert d == D_MODEL and s % FOX_BLK == 0 and s >= 2 * FOX_BLK and (b * s) % IN_TM == 0
    cos_t, sin_t = _rope_tables(s)
    x2 = x.reshape(b * s, d)
    for layer in range(depth):
        w_big, w_small = _layout_w_in(w_in[layer])
        proj, small = _inproj(x2, norm_w[layer][None, :], w_big, w_small)
        kfeat, qfeat_t = _fox_scan(small, _lane_row(f_bias[layer], LANE_F), s)
        ya = _mamba(proj, small, conv_w[layer], conv_b[layer][None, :], _lane_row(dt_bias[layer], 0),
                    _lane_row(a_log[layer], 0), jnp.repeat(d_skip[layer], HEAD_DIM)[None, :],
                    ssm_norm_w[layer][None, :], s)
        yb = _swt_attention(proj, sinks[layer], cos_t, sin_t, s)
        yc = _fox_attention(proj, kfeat, qfeat_t, s)
        x2 = _merge(x2, ya, yb, yc, proj, gate_bias[layer].reshape(1, 3 * d),
                    w_proj[layer].astype(BF16), w_out[layer].astype(BF16), final_norm_w[None, :],
                    final_norm=(layer == depth - 1))
    return x2.reshape(b, s, d)
```
